```python
import jax, jax.numpy as jnp
from jax import lax
import numpy as np

D_MODEL = 1024
BATCH = 8
SEQ = 8192
DEPTH = 1
DEC_BATCH = 128
DEC_SEQ = 1
PAST_LEN = 8192
PAGE_SIZE = 128

SB_HEADS = 8
SB_HEAD_DIM = 64
SB_WIDTH = SB_HEADS * SB_HEAD_DIM
SB_SCALE = SB_HEAD_DIM ** -0.5
SB_BIAS_INIT = -8.0
Q_BLOCK = 128
SG_GROUPS = 8
SG_GROUP_DIM = 64
SG_WIDTH = SG_GROUPS * SG_GROUP_DIM
CHUNK = 128
D_FF = 2816
EPS = 1e-6
POOL_NUM, POOL_DEN = 5, 4

IN_SPLITS = [SB_WIDTH, 2 * SB_WIDTH, 3 * SB_WIDTH, 3 * SB_WIDTH + SG_WIDTH,
             3 * SB_WIDTH + 2 * SG_WIDTH, 3 * SB_WIDTH + 2 * SG_WIDTH + D_MODEL]
IN_WIDTH = 3 * SB_WIDTH + 2 * SG_WIDTH + 2 * D_MODEL

kernel_name = 'gated_parallel_gmlp_stickbreaking_macaron_step'


def _rms_norm(x, g):
    xf = x.astype(jnp.float32)
    y = xf * lax.rsqrt(jnp.mean(xf * xf, axis=-1, keepdims=True) + EPS)
    return (y * g.astype(jnp.float32)).astype(x.dtype)


def _half_ffn(x, g, w_in, w_out):
    gate, up = jnp.split(_rms_norm(x, g) @ w_in, 2, axis=-1)
    return x + 0.5 * ((jax.nn.silu(gate) * up) @ w_out)


def _mixer_inputs(h, g_gv, w_in):
    b, s, _ = h.shape
    q, k, v, u, gv, ga, gb = jnp.split(h @ w_in, IN_SPLITS, axis=-1)
    q = q.reshape(b, s, SB_HEADS, SB_HEAD_DIM)
    k = k.reshape(b, s, SB_HEADS, SB_HEAD_DIM)
    v = v.reshape(b, s, SB_HEADS, SB_HEAD_DIM)
    u = jax.nn.gelu(u, approximate=False)
    gv = _rms_norm(jax.nn.gelu(gv, approximate=False), g_gv)
    return q, k, v, u, gv, jax.nn.sigmoid(ga), jax.nn.sigmoid(gb)


def _spatial_gating(u, gv, w_s, b_s, chunk_len):
    b, s, _ = u.shape
    n_chunks = s // chunk_len
    mask = jnp.tril(jnp.ones((chunk_len, chunk_len), dtype=bool))
    w = jnp.where(mask[None], w_s[:, :chunk_len, :chunk_len], 0.0)
    vg = gv.reshape(b, n_chunks, chunk_len, SG_GROUPS, SG_GROUP_DIM)
    sg = jnp.einsum('gts,bcsgd->bctgd', w, vg) + b_s[:, :chunk_len].T[None, None, :, :, None]
    return u * sg.reshape(b, s, SG_WIDTH)


def _rev_excl_cumsum(x):
    shifted = jnp.concatenate([x[..., 1:], jnp.zeros_like(x[..., :1])], axis=-1)
    return lax.cumsum(shifted, axis=x.ndim - 1, reverse=True)


def _sb_logits(q, k, bias):
    z = jnp.einsum('bqhd,bshd->bhqs', q, k, preferred_element_type=jnp.float32) * SB_SCALE
    return z + bias.astype(jnp.float32)[None, :, None, None]


def _sb_block(z, mask, carry):
    log_keep = jnp.where(mask, jax.nn.log_sigmoid(-z), 0.0)
    log_w = jax.nn.log_sigmoid(z) + _rev_excl_cumsum(log_keep) + carry[..., None]
    w = jnp.where(mask, jnp.exp(log_w), 0.0)
    return w, carry + jnp.sum(log_keep, axis=-1)


def _sb_prompt(q, k, v, bias):
    b, s, h, dh = q.shape
    nq = s // Q_BLOCK
    qb = jnp.moveaxis(q.reshape(b, nq, Q_BLOCK, h, dh), 1, 0)
    key_pos = jnp.arange(s)
    vf = v.astype(jnp.float32)

    def block(args):
        q_blk, blk = args
        z = _sb_logits(q_blk, k, bias)
        q_pos = blk * Q_BLOCK + jnp.arange(Q_BLOCK)
        mask = key_pos[None, :] < q_pos[:, None]
        w, _ = _sb_block(z, mask, jnp.zeros(z.shape[:-1], jnp.float32))
        return jnp.einsum('bhqs,bshd->bqhd', w, vf)

    o = lax.map(block, (qb, jnp.arange(nq)))
    return jnp.moveaxis(o, 0, 1).reshape(b, s, h * dh).astype(q.dtype)


def _sb_sample(q, k, v, cache_k, cache_v, page_table, bias):
    bd, t, h, dh = q.shape
    pos = jnp.arange(t)
    z = _sb_logits(q, k, bias)
    w, carry = _sb_block(z, pos[None, :] < pos[:, None], jnp.zeros((bd, h, t), jnp.float32))
    out = jnp.einsum('bhqs,bshd->bqhd', w, v.astype(jnp.float32))

    def page_step(state, phys):
        acc_out, acc_log = state
        kp = cache_k[phys]
        vp = cache_v[phys]
        zp = _sb_logits(q, kp, bias)
        wp, acc_log = _sb_block(zp, True, acc_log)
        acc_out = acc_out + jnp.einsum('bhqs,bshd->bqhd', wp, vp.astype(jnp.float32))
        return (acc_out, acc_log), None

    (out, _), _ = lax.scan(page_step, (out, carry), page_table.T, reverse=True)
    return out.reshape(bd, t, h * dh).astype(q.dtype)


def _merge(x, y_a, y_b, g_a, g_b, w_a, w_b, w_o):
    return x + (g_a * (y_a @ w_a) + g_b * (y_b @ w_b)) @ w_o


def _normal(key, shape, scale):
    return jax.random.normal(key, shape, jnp.float32) * scale


def setup_inputs(seed: int = 0) -> dict:
    key = jax.random.key(seed)
    ks = jax.random.split(key, 24)
    n_pages = PAST_LEN // PAGE_SIZE
    n_phys = (DEC_BATCH * n_pages * POOL_NUM) // POOL_DEN
    perm = jax.random.permutation(ks[5], n_phys)[:DEC_BATCH * n_pages]
    page_table = perm.reshape(DEC_BATCH, n_pages).astype(jnp.int32)
    tri = jnp.tril(jnp.ones((CHUNK, CHUNK), jnp.float32))
    return {
        'x_prompt': _normal(ks[0], (BATCH, SEQ, D_MODEL), 1.0),
        'x_sample': _normal(ks[1], (DEC_BATCH, DEC_SEQ, D_MODEL), 1.0),
        'cache_k': _normal(ks[2], (DEPTH, n_phys, PAGE_SIZE, SB_HEADS, SB_HEAD_DIM), 1.0),
        'cache_v': _normal(ks[3], (DEPTH, n_phys, PAGE_SIZE, SB_HEADS, SB_HEAD_DIM), 1.0),
        'page_table': page_table,
        'norm_ffn1': 1.0 + _normal(ks[4], (DEPTH, D_MODEL), 0.02),
        'w_ffn1_in': _normal(ks[6], (DEPTH, D_MODEL, 2 * D_FF), D_MODEL ** -0.5),
        'w_ffn1_out': _normal(ks[7], (DEPTH, D_FF, D_MODEL), D_FF ** -0.5),
        'norm_mix': 1.0 + _normal(ks[8], (DEPTH, D_MODEL), 0.02),
        'w_in': _normal(ks[9], (DEPTH, D_MODEL, IN_WIDTH), D_MODEL ** -0.5),
        'norm_gv': 1.0 + _normal(ks[10], (DEPTH, SG_WIDTH), 0.02),
        'sb_bias': SB_BIAS_INIT + _normal(ks[20], (DEPTH, SB_HEADS), 0.1),
        'w_spatial': _normal(ks[11], (DEPTH, SG_GROUPS, CHUNK, CHUNK), CHUNK ** -0.5) * tri,
        'b_spatial': 1.0 + _normal(ks[12], (DEPTH, SG_GROUPS, CHUNK), 0.1),
        'w_branch_a': _normal(ks[13], (DEPTH, SG_WIDTH, D_MODEL), SG_WIDTH ** -0.5),
        'w_branch_b': _normal(ks[14], (DEPTH, SB_WIDTH, D_MODEL), SB_WIDTH ** -0.5),
        'w_out': _normal(ks[15], (DEPTH, D_MODEL, D_MODEL), D_MODEL ** -0.5),
        'norm_ffn2': 1.0 + _normal(ks[16], (DEPTH, D_MODEL), 0.02),
        'w_ffn2_in': _normal(ks[17], (DEPTH, D_MODEL, 2 * D_FF), D_MODEL ** -0.5),
        'w_ffn2_out': _normal(ks[18], (DEPTH, D_FF, D_MODEL), D_FF ** -0.5),
        'norm_final': 1.0 + _normal(ks[19], (D_MODEL,), 0.02),
    }


def reference(x_prompt, x_sample, cache_k, cache_v, page_table, norm_ffn1, w_ffn1_in, w_ffn1_out,
              norm_mix, w_in, norm_gv, sb_bias, w_spatial, b_spatial, w_branch_a, w_branch_b, w_out,
              norm_ffn2, w_ffn2_in, w_ffn2_out, norm_final):
    xp, xs = x_prompt, x_sample
    k_p_rows, v_p_rows, k_s_rows, v_s_rows, gv_s_rows = [], [], [], [], []
    for l in range(DEPTH):
        xp = _half_ffn(xp, norm_ffn1[l], w_ffn1_in[l], w_ffn1_out[l])
        xs = _half_ffn(xs, norm_ffn1[l], w_ffn1_in[l], w_ffn1_out[l])
        qp, kp, vp, up, gvp, gap, gbp = _mixer_inputs(_rms_norm(xp, norm_mix[l]), norm_gv[l], w_in[l])
        qs, ks_, vs, us, gvs, gas, gbs = _mixer_inputs(_rms_norm(xs, norm_mix[l]), norm_gv[l], w_in[l])
        ya_p = _spatial_gating(up, gvp, w_spatial[l], b_spatial[l], CHUNK)
        ya_s = _spatial_gating(us, gvs, w_spatial[l], b_spatial[l], DEC_SEQ)
        yb_p = _sb_prompt(qp, kp, vp, sb_bias[l])
        yb_s = _sb_sample(qs, ks_, vs, cache_k[l], cache_v[l], page_table, sb_bias[l])
        xp = _merge(xp, ya_p, yb_p, gap, gbp, w_branch_a[l], w_branch_b[l], w_out[l])
        xs = _merge(xs, ya_s, yb_s, gas, gbs, w_branch_a[l], w_branch_b[l], w_out[l])
        xp = _half_ffn(xp, norm_ffn2[l], w_ffn2_in[l], w_ffn2_out[l])
        xs = _half_ffn(xs, norm_ffn2[l], w_ffn2_in[l], w_ffn2_out[l])
        k_p_rows.append(kp)
        v_p_rows.append(vp)
        k_s_rows.append(ks_)
        v_s_rows.append(vs)
        gv_s_rows.append(gvs)
    y_prompt = _rms_norm(xp, norm_final)
    y_sample = _rms_norm(xs, norm_final)
    return (y_prompt, y_sample, jnp.stack(k_p_rows), jnp.stack(v_p_rows),
            jnp.stack(k_s_rows), jnp.stack(v_s_rows), jnp.stack(gv_s_rows))
```

```python
import functools

import jax
import jax.numpy as jnp
from jax import lax
from jax.experimental import pallas as pl
from jax.experimental.pallas import tpu as pltpu

D_MODEL = 1024
SB_HEADS = 8
SB_HEAD_DIM = 64
SB_WIDTH = SB_HEADS * SB_HEAD_DIM
SB_SCALE = SB_HEAD_DIM ** -0.5
SG_GROUPS = 8
SG_GROUP_DIM = 64
SG_WIDTH = SG_GROUPS * SG_GROUP_DIM
CHUNK = 128
PAGE_SIZE = 128
D_FF = 2816
EPS = 1e-6

LANES = 128
VMEM_LIMIT_BYTES = 56 * 1024 * 1024

F32 = jnp.float32
BF16 = jnp.bfloat16


def _rms(x, g):
    return (x * lax.rsqrt(jnp.mean(x * x, axis=-1, keepdims=True) + EPS)) * g


def _dot(a, b):
    return jnp.dot(a, b, preferred_element_type=F32)


def _gelu(x):
    return 0.5 * x * (1.0 + lax.erf(x * (2.0 ** -0.5)))


def _log_sigmoid_pair(z):
    ls = jnp.minimum(z, 0.0) - jnp.log1p(jnp.exp(-jnp.abs(z)))
    return ls, ls - z


def _const_spec(shape):
    zeros = (0,) * len(shape)
    return pl.BlockSpec(shape, lambda *_: zeros)


def _ffn_kernel(x_ref, g_ref, win_ref, wout_ref, *rest, chunks, final_norm):
    if final_norm:
        gf_ref, o_ref = rest
    else:
        (o_ref,) = rest
    x = x_ref[...]
    xn = _rms(x, g_ref[...]).astype(BF16)
    acc = jnp.zeros(x.shape, F32)
    for lo, hi in chunks:
        gate = _dot(xn, win_ref[:, lo:hi])
        up = _dot(xn, win_ref[:, D_FF + lo:D_FF + hi])
        act = (gate * jax.nn.sigmoid(gate) * up).astype(BF16)
        acc = acc + _dot(act, wout_ref[lo:hi, :])
    y = x + 0.5 * acc
    if final_norm:
        y = _rms(y, gf_ref[...])
    o_ref[...] = y


def _ffn(x, g, w_in, w_out, g_final=None, *, tm):
    n = x.shape[0]
    chunks = ((0, 1536), (1536, D_FF))
    final_norm = g_final is not None
    in_specs = [
        pl.BlockSpec((tm, D_MODEL), lambda i: (i, 0)),
        _const_spec((1, D_MODEL)),
        _const_spec((D_MODEL, 2 * D_FF)),
        _const_spec((D_FF, D_MODEL)),
    ]
    args = [x, g.reshape(1, D_MODEL), w_in, w_out]
    if final_norm:
        in_specs.append(_const_spec((1, D_MODEL)))
        args.append(g_final.reshape(1, D_MODEL))
    return pl.pallas_call(
        functools.partial(_ffn_kernel, chunks=chunks, final_norm=final_norm),
        grid=(n // tm,),
        in_specs=in_specs,
        out_specs=pl.BlockSpec((tm, D_MODEL), lambda i: (i, 0)),
        out_shape=jax.ShapeDtypeStruct((n, D_MODEL), F32),
        compiler_params=pltpu.CompilerParams(
            dimension_semantics=("arbitrary",), vmem_limit_bytes=VMEM_LIMIT_BYTES),
        name="ffn_final" if final_norm else "ffn",
    )(*args)


def _proj_kernel(x_ref, g_ref, w_ref, ggv_ref, sg_w_ref, sg_b_ref,
                 q_ref, k_ref, v_ref, kb_ref, vb_ref, ya_ref, ga_ref, gb_ref, *rest, chunked):
    tm = x_ref.shape[0]
    h = _rms(x_ref[...], g_ref[...]).astype(BF16)

    def seg(lo, width):
        return _dot(h, w_ref[:, lo:lo + width])

    q_ref[...] = (seg(0, SB_WIDTH) * SB_SCALE).astype(q_ref.dtype)
    k = seg(SB_WIDTH, SB_WIDTH)
    k_ref[...] = k
    kb_ref[...] = k.astype(BF16)
    v = seg(2 * SB_WIDTH, SB_WIDTH)
    v_ref[...] = v
    vb_ref[...] = v.astype(BF16)
    u = _gelu(seg(3 * SB_WIDTH, SG_WIDTH))
    gv = _rms(_gelu(seg(3 * SB_WIDTH + SG_WIDTH, SG_WIDTH)), ggv_ref[...])
    base = 3 * SB_WIDTH + 2 * SG_WIDTH
    ga_ref[...] = jax.nn.sigmoid(seg(base, D_MODEL)).astype(BF16)
    gb_ref[...] = jax.nn.sigmoid(seg(base + D_MODEL, D_MODEL)).astype(BF16)

    if chunked:
        gvb = gv.astype(BF16)
        row = lax.broadcasted_iota(jnp.int32, (2 * CHUNK, CHUNK), 0)
        col = lax.broadcasted_iota(jnp.int32, (2 * CHUNK, CHUNK), 1)
        causal = col <= jnp.where(row >= CHUNK, row - CHUNK, row)
        lane = lax.broadcasted_iota(jnp.int32, (CHUNK, LANES), 1)
        first_group = lane < SG_GROUP_DIM
        for p in range(SG_WIDTH // LANES):
            w_pair = jnp.where(causal, sg_w_ref[p], 0.0).astype(BF16)
            cols = slice(p * LANES, (p + 1) * LANES)
            for c in range(tm // CHUNK):
                rows = slice(c * CHUNK, (c + 1) * CHUNK)
                r = _dot(w_pair, gvb[rows, cols])
                sg = jnp.where(first_group, r[:CHUNK], r[CHUNK:]) + sg_b_ref[:, cols]
                ya_ref[rows, cols] = (u[rows, cols] * sg).astype(BF16)
    else:
        (gv_ref,) = rest
        gv_ref[...] = gv
        ya_ref[...] = (u * (gv * sg_w_ref[...] + sg_b_ref[...])).astype(BF16)


def _proj(x, g, w, g_gv, sg_w, sg_b, *, tm, chunked):
    n = x.shape[0]
    row_spec = lambda width: pl.BlockSpec((tm, width), lambda i: (i, 0))
    out_shape = [
        jax.ShapeDtypeStruct((n, SB_WIDTH), BF16 if chunked else F32),
        jax.ShapeDtypeStruct((n, SB_WIDTH), F32),
        jax.ShapeDtypeStruct((n, SB_WIDTH), F32),
        jax.ShapeDtypeStruct((n, SB_WIDTH), BF16),
        jax.ShapeDtypeStruct((n, SB_WIDTH), BF16),
        jax.ShapeDtypeStruct((n, SG_WIDTH), BF16),
        jax.ShapeDtypeStruct((n, D_MODEL), BF16),
        jax.ShapeDtypeStruct((n, D_MODEL), BF16),
    ]
    out_specs = [row_spec(SB_WIDTH)] * 5 + [row_spec(SG_WIDTH), row_spec(D_MODEL), row_spec(D_MODEL)]
    if not chunked:
        out_shape.append(jax.ShapeDtypeStruct((n, SG_WIDTH), F32))
        out_specs.append(row_spec(SG_WIDTH))
    return pl.pallas_call(
        functools.partial(_proj_kernel, chunked=chunked),
        grid=(n // tm,),
        in_specs=[
            row_spec(D_MODEL),
            _const_spec((1, D_MODEL)),
            _const_spec(w.shape),
            _const_spec((1, SG_WIDTH)),
            _const_spec(sg_w.shape),
            _const_spec(sg_b.shape),
        ],
        out_specs=out_specs,
        out_shape=out_shape,
        compiler_params=pltpu.CompilerParams(
            dimension_semantics=("arbitrary",), vmem_limit_bytes=VMEM_LIMIT_BYTES),
        name="proj_prompt" if chunked else "proj_sample",
    )(x, g.reshape(1, D_MODEL), w, g_gv.reshape(1, SG_WIDTH), sg_w, sg_b)


def _merge_kernel(x_ref, ya_ref, yb_ref, ga_ref, gb_ref, wa_ref, wb_ref, wo_ref, o_ref):
    m = (ga_ref[...].astype(F32) * _dot(ya_ref[...], wa_ref[...])
         + gb_ref[...].astype(F32) * _dot(yb_ref[...], wb_ref[...]))
    o_ref[...] = x_ref[...] + _dot(m.astype(BF16), wo_ref[...])


def _merge(x, ya, yb, ga, gb, wa, wb, wo, *, tm):
    n = x.shape[0]
    row_spec = lambda width: pl.BlockSpec((tm, width), lambda i: (i, 0))
    return pl.pallas_call(
        _merge_kernel,
        grid=(n // tm,),
        in_specs=[row_spec(D_MODEL), row_spec(SG_WIDTH), row_spec(SB_WIDTH), row_spec(D_MODEL),
                  row_spec(D_MODEL), _const_spec(wa.shape), _const_spec(wb.shape), _const_spec(wo.shape)],
        out_specs=row_spec(D_MODEL),
        out_shape=jax.ShapeDtypeStruct((n, D_MODEL), F32),
        compiler_params=pltpu.CompilerParams(
            dimension_semantics=("arbitrary",), vmem_limit_bytes=VMEM_LIMIT_BYTES),
        name="merge",
    )(x, ya, yb, ga, gb, wa, wb, wo)


ATTN_BLOCK = 256


def _attn_prompt_kernel(bias_ref, q_ref, k_ref, v_ref, tri_ref, o_ref):
    t = ATTN_BLOCK
    hp = pl.program_id(1)
    qi = pl.program_id(2)
    q = q_ref[0]
    lane = lax.broadcasted_iota(jnp.int32, (t, LANES), 1)
    first_head = lane < SB_HEAD_DIM
    zero = jnp.zeros_like(q)
    q_heads = (jnp.where(first_head, q, zero), jnp.where(first_head, zero, q))
    biases = (bias_ref[2 * hp], bias_ref[2 * hp + 1])
    tri = tri_ref[...]
    row = lax.broadcasted_iota(jnp.int32, (t, t), 0)
    col = lax.broadcasted_iota(jnp.int32, (t, t), 1)
    strictly_earlier = col < row

    def block(j, state, masked):
        start = pl.multiple_of(j * t, t)
        kb = k_ref[0, pl.ds(start, t), :]
        vb = v_ref[0, pl.ds(start, t), :]
        new_state = []
        for qh, bias, (carry, acc) in zip(q_heads, biases, state):
            z = lax.dot_general(qh, kb, (((1,), (1,)), ((), ())), preferred_element_type=F32) + bias
            ls, log_keep = _log_sigmoid_pair(z)
            if masked:
                log_keep = jnp.where(strictly_earlier, log_keep, 0.0)
            later = _dot(log_keep.astype(BF16), tri)
            w = jnp.exp(ls + later + carry)
            if masked:
                w = jnp.where(strictly_earlier, w, 0.0)
            acc = acc + _dot(w.astype(BF16), vb)
            carry = carry + jnp.sum(log_keep, axis=-1, keepdims=True)
            new_state.append((carry, acc))
        return tuple(new_state)

    init = tuple((jnp.zeros((t, 1), F32), jnp.zeros((t, LANES), F32)) for _ in range(2))
    state = block(qi, init, masked=True)
    state = lax.fori_loop(1, qi + 1, lambda i, s: block(qi - i, s, masked=False), state)
    o_ref[0] = jnp.where(first_head, state[0][1], state[1][1]).astype(o_ref.dtype)


def _attn_prompt(q, k, v, bias):
    b, s, _ = q.shape
    t = ATTN_BLOCK
    idx = jnp.arange(t)
    tri = (idx[:, None] > idx[None, :]).astype(BF16)
    return pl.pallas_call(
        _attn_prompt_kernel,
        grid=(b, SB_WIDTH // LANES, s // t),
        in_specs=[
            pl.BlockSpec(memory_space=pltpu.SMEM),
            pl.BlockSpec((1, t, LANES), lambda bi, hp, qi: (bi, qi, hp)),
            pl.BlockSpec((1, s, LANES), lambda bi, hp, qi: (bi, 0, hp)),
            pl.BlockSpec((1, s, LANES), lambda bi, hp, qi: (bi, 0, hp)),
            _const_spec((t, t)),
        ],
        out_specs=pl.BlockSpec((1, t, LANES), lambda bi, hp, qi: (bi, qi, hp)),
        out_shape=jax.ShapeDtypeStruct((b, s, SB_WIDTH), BF16),
        compiler_params=pltpu.CompilerParams(
            dimension_semantics=("arbitrary", "arbitrary", "arbitrary"),
            vmem_limit_bytes=VMEM_LIMIT_BYTES),
        name="attn_prompt",
    )(bias, q, k, v, tri)


PAGES_PER_STEP = 8


def _attn_sample_kernel(pt_ref, q_ref, kn_ref, vn_ref, bias_ref, tri_ref, *refs):
    del pt_ref
    g = PAGES_PER_STEP
    k_refs, v_refs = refs[:g], refs[g:2 * g]
    o_ref, qcol_ref, carry_ref, acc_ref = refs[2 * g:]
    step = pl.program_id(1)
    bias = bias_ref[...]

    def columns(row_ref):
        wide = jnp.broadcast_to(row_ref[0], (LANES, SB_WIDTH)).T
        return wide.reshape(SB_HEADS, SB_HEAD_DIM, LANES)

    head_row = lax.broadcasted_iota(jnp.int32, (SB_HEADS, LANES), 0)

    def head_rows(prod):
        out = jnp.zeros((SB_HEADS, LANES), F32)
        for h in range(SB_HEADS):
            out = jnp.where(head_row == h, jnp.sum(prod[h], axis=0, keepdims=True), out)
        return out

    @pl.when(step == 0)
    def _():
        qcol = columns(q_ref)
        qcol_ref[...] = qcol
        lane = lax.broadcasted_iota(jnp.int32, (SB_HEADS, LANES), 1)
        pos = jnp.zeros_like(lane)
        mask = (pos < pos) & (lane == 0)
        z = head_rows(qcol * columns(kn_ref)) + bias
        ls, log_keep = _log_sigmoid_pair(z)
        log_keep = jnp.where(mask, log_keep, 0.0)
        w = jnp.where(mask, jnp.exp(ls), 0.0)
        vcol = columns(vn_ref)
        for h in range(SB_HEADS):
            acc_ref[h] = w[h:h + 1, :] * vcol[h]
        carry_ref[...] = jnp.broadcast_to(jnp.sum(log_keep, axis=-1, keepdims=True), (SB_HEADS, LANES))

    qcol = qcol_ref[...]
    z = jnp.concatenate([head_rows(k_refs[i][0] * qcol) for i in range(g)], axis=0)
    z = z + jnp.concatenate([bias] * g, axis=0)
    ls, log_keep = _log_sigmoid_pair(z)
    later = _dot(log_keep.astype(BF16), tri_ref[...])
    page_sum = jnp.sum(log_keep, axis=-1, keepdims=True)
    carry = carry_ref[...]
    weights = []
    for i in range(g):
        rows = slice(i * SB_HEADS, (i + 1) * SB_HEADS)
        weights.append(jnp.exp(ls[rows] + later[rows] + carry))
        carry = carry + page_sum[rows]
    carry_ref[...] = carry
    for h in range(SB_HEADS):
        acc = acc_ref[h]
        for i in range(g):
            acc = acc + v_refs[i][0, h] * weights[i][h:h + 1, :]
        acc_ref[h] = acc

    @pl.when(step == pl.num_programs(1) - 1)
    def _():
        lane = lax.broadcasted_iota(jnp.int32, (SB_HEAD_DIM, LANES), 1)
        out = jnp.zeros((SB_HEAD_DIM, LANES), F32)
        for h in range(SB_HEADS):
            out = jnp.where(lane == h, jnp.sum(acc_ref[h], axis=-1, keepdims=True), out)
        o_ref[0] = out


def _attn_sample(q, k_new, v_new, cache_k, cache_v, page_table, bias):
    bd = q.shape[0]
    n_pages = page_table.shape[1]
    g = PAGES_PER_STEP
    ck = jnp.transpose(cache_k, (0, 2, 3, 1))
    cv = jnp.transpose(cache_v, (0, 2, 3, 1))
    idx = jnp.arange(PAGE_SIZE)
    tri = (idx[:, None] > idx[None, :]).astype(BF16)
    bias_rows = jnp.broadcast_to(bias.reshape(SB_HEADS, 1), (SB_HEADS, LANES))
    row3 = lambda a: a.reshape(bd, 1, SB_WIDTH)
    sample_spec = pl.BlockSpec((1, 1, SB_WIDTH), lambda b, s, pt: (b, 0, 0))
    const = lambda shape: pl.BlockSpec(shape, lambda b, s, pt: (0,) * len(shape))

    def page_spec(i):
        return pl.BlockSpec((1, SB_HEADS, SB_HEAD_DIM, PAGE_SIZE),
                            lambda b, s, pt: (pt[b, n_pages - 1 - (s * g + i)], 0, 0, 0))

    pages = [page_spec(i) for i in range(g)]
    out = pl.pallas_call(
        _attn_sample_kernel,
        grid_spec=pltpu.PrefetchScalarGridSpec(
            num_scalar_prefetch=1,
            grid=(bd, n_pages // g),
            in_specs=[sample_spec, sample_spec, sample_spec, const((SB_HEADS, LANES)),
                      const((PAGE_SIZE, PAGE_SIZE))] + pages + pages,
            out_specs=pl.BlockSpec((1, SB_HEAD_DIM, LANES), lambda b, s, pt: (b, 0, 0)),
            scratch_shapes=[pltpu.VMEM((SB_HEADS, SB_HEAD_DIM, LANES), F32),
                            pltpu.VMEM((SB_HEADS, LANES), F32),
                            pltpu.VMEM((SB_HEADS, SB_HEAD_DIM, PAGE_SIZE), F32)],
        ),
        out_shape=jax.ShapeDtypeStruct((bd, SB_HEAD_DIM, LANES), F32),
        compiler_params=pltpu.CompilerParams(
            dimension_semantics=("arbitrary", "arbitrary"), vmem_limit_bytes=VMEM_LIMIT_BYTES),
        name="attn_sample",
    )(page_table, row3(q), row3(k_new), row3(v_new), bias_rows, tri, *([ck] * g), *([cv] * g))
    return jnp.transpose(out[:, :, :SB_HEADS], (0, 2, 1)).reshape(bd, SB_WIDTH)


def kernel(x_prompt, x_sample, cache_k, cache_v, page_table, norm_ffn1, w_ffn1_in, w_ffn1_out, norm_mix, w_in, norm_gv, sb_bias, w_spatial, b_spatial, w_branch_a, w_branch_b, w_out, norm_ffn2, w_ffn2_in, w_ffn2_out, norm_final):
    b, s, _ = x_prompt.shape
    bd, t, _ = x_sample.shape
    depth = norm_ffn1.shape[0]
    xp = x_prompt.reshape(b * s, D_MODEL)
    xs = x_sample.reshape(bd * t, D_MODEL)
    tm_p, tm_s = 512, bd * t
    outs = [[] for _ in range(5)]
    for l in range(depth):
        w1i, w1o = w_ffn1_in[l].astype(BF16), w_ffn1_out[l].astype(BF16)
        w2i, w2o = w_ffn2_in[l].astype(BF16), w_ffn2_out[l].astype(BF16)
        wi = w_in[l].astype(BF16)
        wa, wb, wo = w_branch_a[l].astype(BF16), w_branch_b[l].astype(BF16), w_out[l].astype(BF16)
        last = l == depth - 1
        sg_w_p = w_spatial[l][:, :CHUNK, :CHUNK].reshape(SG_GROUPS // 2, 2 * CHUNK, CHUNK)
        sg_b_p = jnp.repeat(b_spatial[l][:, :CHUNK].T, SG_GROUP_DIM, axis=1)
        sg_w_s = jnp.repeat(w_spatial[l][:, 0, 0], SG_GROUP_DIM).reshape(1, SG_WIDTH)
        sg_b_s = jnp.repeat(b_spatial[l][:, 0], SG_GROUP_DIM).reshape(1, SG_WIDTH)

        xp = _ffn(xp, norm_ffn1[l], w1i, w1o, tm=tm_p)
        xs = _ffn(xs, norm_ffn1[l], w1i, w1o, tm=tm_s)

        qp, kp, vp, kpb, vpb, ya_p, ga_p, gb_p = _proj(
            xp, norm_mix[l], wi, norm_gv[l], sg_w_p, sg_b_p, tm=tm_p, chunked=True)
        qs, ks, vs, _, _, ya_s, ga_s, gb_s, gvs = _proj(
            xs, norm_mix[l], wi, norm_gv[l], sg_w_s, sg_b_s, tm=tm_s, chunked=False)

        yb_p = _attn_prompt(qp.reshape(b, s, SB_WIDTH), kpb.reshape(b, s, SB_WIDTH),
                            vpb.reshape(b, s, SB_WIDTH), sb_bias[l]).reshape(b * s, SB_WIDTH)
        yb_s = _attn_sample(qs, ks, vs, cache_k[l], cache_v[l], page_table, sb_bias[l]).astype(BF16)

        xp = _merge(xp, ya_p, yb_p, ga_p, gb_p, wa, wb, wo, tm=tm_p)
        xs = _merge(xs, ya_s, yb_s, ga_s, gb_s, wa, wb, wo, tm=tm_s)

        xp = _ffn(xp, norm_ffn2[l], w2i, w2o, norm_final if last else None, tm=tm_p)
        xs = _ffn(xs, norm_ffn2[l], w2i, w2o, norm_final if last else None, tm=tm_s)

        outs[0].append(kp.reshape(b, s, SB_HEADS, SB_HEAD_DIM))
        outs[1].append(vp.reshape(b, s, SB_HEADS, SB_HEAD_DIM))
        outs[2].append(ks.reshape(bd, t, SB_HEADS, SB_HEAD_DIM))
        outs[3].append(vs.reshape(bd, t, SB_HEADS, SB_HEAD_DIM))
        outs[4].append(gvs.reshape(bd, t, SG_WIDTH))
    return (xp.reshape(b, s, D_MODEL), xs.reshape(bd, t, D_MODEL),
            jnp.stack(outs[0]), jnp.stack(outs[1]), jnp.stack(outs[2]), jnp.stack(outs[3]),
            jnp.stack(outs[4]))
```

```python
import functools

import jax
import jax.numpy as jnp
from jax import lax
from jax.experimental import pallas as pl
from jax.experimental.pallas import tpu as pltpu

D_MODEL = 1024
SB_HEADS = 8
SB_HEAD_DIM = 64
SB_WIDTH = SB_HEADS * SB_HEAD_DIM
SB_SCALE = SB_HEAD_DIM ** -0.5
SG_GROUPS = 8
SG_GROUP_DIM = 64
SG_WIDTH = SG_GROUPS * SG_GROUP_DIM
CHUNK = 128
PAGE_SIZE = 128
D_FF = 2816
EPS = 1e-6
LOG2E = 1.4426950408889634

ATTN_BLOCK = 256
HEADS_PER_STEP = 4
GROUP_LANES = HEADS_PER_STEP * SB_HEAD_DIM
SIGN_BIT = 0x80000000
PAGES_PER_STEP = 16

LANES = 128
VMEM_LIMIT_BYTES = 56 * 1024 * 1024

F32 = jnp.float32
BF16 = jnp.bfloat16


def _rms(x, g):
    return (x * lax.rsqrt(jnp.mean(x * x, axis=-1, keepdims=True) + EPS)) * g


def _dot(a, b):
    return jnp.dot(a, b, preferred_element_type=F32)


def _gelu(x):
    return 0.5 * x * (1.0 + lax.erf(x * (2.0 ** -0.5)))


def _log_sigmoid_pair(z):
    ls = jnp.minimum(z, 0.0) - jnp.log1p(jnp.exp(-jnp.abs(z)))
    return ls, ls - z


def _const_spec(shape):
    zeros = (0,) * len(shape)
    return pl.BlockSpec(shape, lambda *_: zeros)


def _ffn_kernel(x_ref, g_ref, win_ref, wout_ref, *rest, chunks, final_norm):
    if final_norm:
        gf_ref, o_ref = rest
    else:
        (o_ref,) = rest
    x = x_ref[...]
    xn = _rms(x, g_ref[...]).astype(BF16)
    acc = jnp.zeros(x.shape, F32)
    for lo, hi in chunks:
        gate = _dot(xn, win_ref[:, lo:hi])
        up = _dot(xn, win_ref[:, D_FF + lo:D_FF + hi])
        act = (gate * jax.nn.sigmoid(gate) * up).astype(BF16)
        acc = acc + _dot(act, wout_ref[lo:hi, :])
    y = x + 0.5 * acc
    if final_norm:
        y = _rms(y, gf_ref[...])
    o_ref[...] = y


def _ffn(x, g, w_in, w_out, g_final=None, *, tm):
    n = x.shape[0]
    chunks = ((0, 1536), (1536, D_FF))
    final_norm = g_final is not None
    in_specs = [
        pl.BlockSpec((tm, D_MODEL), lambda i: (i, 0)),
        _const_spec((1, D_MODEL)),
        _const_spec((D_MODEL, 2 * D_FF)),
        _const_spec((D_FF, D_MODEL)),
    ]
    args = [x, g.reshape(1, D_MODEL), w_in, w_out]
    if final_norm:
        in_specs.append(_const_spec((1, D_MODEL)))
        args.append(g_final.reshape(1, D_MODEL))
    return pl.pallas_call(
        functools.partial(_ffn_kernel, chunks=chunks, final_norm=final_norm),
        grid=(n // tm,),
        in_specs=in_specs,
        out_specs=pl.BlockSpec((tm, D_MODEL), lambda i: (i, 0)),
        out_shape=jax.ShapeDtypeStruct((n, D_MODEL), F32),
        compiler_params=pltpu.CompilerParams(
            dimension_semantics=("arbitrary",), vmem_limit_bytes=VMEM_LIMIT_BYTES),
        name="ffn_final" if final_norm else "ffn",
    )(*args)


def _mixer_inputs(x_ref, g_ref, w_ref, ggv_ref, ga_ref, gb_ref):
    h = _rms(x_ref[...], g_ref[...]).astype(BF16)

    def seg(lo, width):
        return _dot(h, w_ref[:, lo:lo + width])

    q = seg(0, SB_WIDTH)
    k = seg(SB_WIDTH, SB_WIDTH)
    v = seg(2 * SB_WIDTH, SB_WIDTH)
    u = _gelu(seg(3 * SB_WIDTH, SG_WIDTH))
    gv = _rms(_gelu(seg(3 * SB_WIDTH + SG_WIDTH, SG_WIDTH)), ggv_ref[...])
    base = 3 * SB_WIDTH + 2 * SG_WIDTH
    ga_ref[...] = jax.nn.sigmoid(seg(base, D_MODEL)).astype(BF16)
    gb_ref[...] = jax.nn.sigmoid(seg(base + D_MODEL, D_MODEL)).astype(BF16)
    return q, k, v, u, gv


def _proj_prompt_kernel(x_ref, g_ref, w_ref, ggv_ref, sg_w_ref, sg_b_ref,
                        q_ref, kt_ref, vt_ref, ktb_ref, vb_ref, ya_ref, ga_ref, gb_ref):
    tm = x_ref.shape[0]
    q, k, v, u, gv = _mixer_inputs(x_ref, g_ref, w_ref, ggv_ref, ga_ref, gb_ref)
    q_ref[...] = (q * (SB_SCALE * LOG2E)).astype(BF16)
    kt = k.T
    kt_ref[0] = kt
    for j in range(tm // ATTN_BLOCK):
        ktb_ref[0, j] = kt[:, j * ATTN_BLOCK:(j + 1) * ATTN_BLOCK].astype(BF16)
    vt_ref[0] = v.T
    vb_ref[...] = v.astype(BF16)

    gvb = gv.astype(BF16)
    row = lax.broadcasted_iota(jnp.int32, (2 * CHUNK, CHUNK), 0)
    col = lax.broadcasted_iota(jnp.int32, (2 * CHUNK, CHUNK), 1)
    causal = col <= jnp.where(row >= CHUNK, row - CHUNK, row)
    lane = lax.broadcasted_iota(jnp.int32, (CHUNK, LANES), 1)
    first_group = lane < SG_GROUP_DIM
    for p in range(SG_WIDTH // LANES):
        w_pair = jnp.where(causal, sg_w_ref[p], 0.0).astype(BF16)
        cols = slice(p * LANES, (p + 1) * LANES)
        for c in range(tm // CHUNK):
            rows = slice(c * CHUNK, (c + 1) * CHUNK)
            r = _dot(w_pair, gvb[rows, cols])
            sg = jnp.where(first_group, r[:CHUNK], r[CHUNK:]) + sg_b_ref[:, cols]
            ya_ref[rows, cols] = (u[rows, cols] * sg).astype(BF16)


def _proj_sample_kernel(x_ref, g_ref, w_ref, ggv_ref, sg_w_ref, sg_b_ref,
                        q_ref, k_ref, v_ref, ya_ref, ga_ref, gb_ref, gv_ref):
    q, k, v, u, gv = _mixer_inputs(x_ref, g_ref, w_ref, ggv_ref, ga_ref, gb_ref)
    q_ref[...] = q * SB_SCALE
    k_ref[...] = k
    v_ref[...] = v
    gv_ref[...] = gv
    ya_ref[...] = (u * (gv * sg_w_ref[...] + sg_b_ref[...])).astype(BF16)


def _proj_in_specs(tm, w, sg_w, sg_b):
    return [
        pl.BlockSpec((tm, D_MODEL), lambda i: (i, 0)),
        _const_spec((1, D_MODEL)),
        _const_spec(w.shape),
        _const_spec((1, SG_WIDTH)),
        _const_spec(sg_w.shape),
        _const_spec(sg_b.shape),
    ]


def _proj_prompt(x, g, w, g_gv, sg_w, sg_b, *, batch, tm):
    n = x.shape[0]
    s = n // batch
    tiles = s // tm
    t = ATTN_BLOCK
    row_spec = lambda width: pl.BlockSpec((tm, width), lambda i: (i, 0))
    transposed_spec = pl.BlockSpec((1, SB_WIDTH, tm), lambda i: (i // tiles, 0, i % tiles))
    out_shape = [
        jax.ShapeDtypeStruct((n, SB_WIDTH), BF16),
        jax.ShapeDtypeStruct((batch, SB_WIDTH, s), F32),
        jax.ShapeDtypeStruct((batch, SB_WIDTH, s), F32),
        jax.ShapeDtypeStruct((batch, s // t, SB_WIDTH, t), BF16),
        jax.ShapeDtypeStruct((n, SB_WIDTH), BF16),
        jax.ShapeDtypeStruct((n, SG_WIDTH), BF16),
        jax.ShapeDtypeStruct((n, D_MODEL), BF16),
        jax.ShapeDtypeStruct((n, D_MODEL), BF16),
    ]
    out_specs = [
        row_spec(SB_WIDTH), transposed_spec, transposed_spec,
        pl.BlockSpec((1, tm // t, SB_WIDTH, t), lambda i: (i // tiles, i % tiles, 0, 0)),
        row_spec(SB_WIDTH), row_spec(SG_WIDTH), row_spec(D_MODEL), row_spec(D_MODEL),
    ]
    return pl.pallas_call(
        _proj_prompt_kernel,
        grid=(n // tm,),
        in_specs=_proj_in_specs(tm, w, sg_w, sg_b),
        out_specs=out_specs,
        out_shape=out_shape,
        compiler_params=pltpu.CompilerParams(
            dimension_semantics=("arbitrary",), vmem_limit_bytes=VMEM_LIMIT_BYTES),
        name="proj_prompt",
    )(x, g.reshape(1, D_MODEL), w, g_gv.reshape(1, SG_WIDTH), sg_w, sg_b)


def _proj_sample(x, g, w, g_gv, sg_w, sg_b, *, tm):
    n = x.shape[0]
    row_spec = lambda width: pl.BlockSpec((tm, width), lambda i: (i, 0))
    widths = [SB_WIDTH, SB_WIDTH, SB_WIDTH, SG_WIDTH, D_MODEL, D_MODEL, SG_WIDTH]
    dtypes = [F32, F32, F32, BF16, BF16, BF16, F32]
    return pl.pallas_call(
        _proj_sample_kernel,
        grid=(n // tm,),
        in_specs=_proj_in_specs(tm, w, sg_w, sg_b),
        out_specs=[row_spec(wd) for wd in widths],
        out_shape=[jax.ShapeDtypeStruct((n, wd), dt) for wd, dt in zip(widths, dtypes)],
        compiler_params=pltpu.CompilerParams(
            dimension_semantics=("arbitrary",), vmem_limit_bytes=VMEM_LIMIT_BYTES),
        name="proj_sample",
    )(x, g.reshape(1, D_MODEL), w, g_gv.reshape(1, SG_WIDTH), sg_w, sg_b)


def _merge_kernel(x_ref, ya_ref, yb_ref, ga_ref, gb_ref, wa_ref, wb_ref, wo_ref, o_ref):
    m = (ga_ref[...].astype(F32) * _dot(ya_ref[...], wa_ref[...])
         + gb_ref[...].astype(F32) * _dot(yb_ref[...], wb_ref[...]))
    o_ref[...] = x_ref[...] + _dot(m.astype(BF16), wo_ref[...])


def _merge(x, ya, yb, ga, gb, wa, wb, wo, *, tm):
    n = x.shape[0]
    row_spec = lambda width: pl.BlockSpec((tm, width), lambda i: (i, 0))
    return pl.pallas_call(
        _merge_kernel,
        grid=(n // tm,),
        in_specs=[row_spec(D_MODEL), row_spec(SG_WIDTH), row_spec(SB_WIDTH), row_spec(D_MODEL),
                  row_spec(D_MODEL), _const_spec(wa.shape), _const_spec(wb.shape), _const_spec(wo.shape)],
        out_specs=row_spec(D_MODEL),
        out_shape=jax.ShapeDtypeStruct((n, D_MODEL), F32),
        compiler_params=pltpu.CompilerParams(
            dimension_semantics=("arbitrary",), vmem_limit_bytes=VMEM_LIMIT_BYTES),
        name="merge",
    )(x, ya, yb, ga, gb, wa, wb, wo)


def _attn_prompt_kernel(bias_ref, q_ref, kt_ref, v_ref, tri_ref, o_ref,
                        qs_ref, s_ref, nk_ref, lb_ref, ca_ref, lt_ref, w_ref, vs_ref,
                        acc_ref, run_ref):
    t = ATTN_BLOCK
    nh = HEADS_PER_STEP
    grp = pl.program_id(1)
    qi = pl.program_id(2)
    n_blocks = qi + 1
    lane_head = lax.broadcasted_iota(jnp.int32, (t, GROUP_LANES), 1) // SB_HEAD_DIM
    biases = [bias_ref[nh * grp + h] * LOG2E for h in range(nh)]
    row = lax.broadcasted_iota(jnp.int32, (t, t), 0)
    col = lax.broadcasted_iota(jnp.int32, (t, t), 1)
    strictly_earlier = col < row
    head_rows = [slice(h * t, (h + 1) * t) for h in range(nh)]

    q = q_ref[0]
    zero = jnp.zeros_like(q)
    for h, rows in enumerate(head_rows):
        qs_ref[rows, :] = jnp.where(lane_head == h, q, zero)
    acc_ref[...] = jnp.zeros_like(acc_ref)
    run_ref[...] = jnp.zeros_like(run_ref)

    def stage_a0(n, p):
        s_ref[p] = _dot(qs_ref[...], kt_ref[0, qi - n])

    def stage_a1(p, masked):
        for h, rows in enumerate(head_rows):
            z = s_ref[p, rows, :] + biases[h]
            minus_abs = lax.bitcast_convert_type(
                lax.bitcast_convert_type(z, jnp.uint32) | jnp.uint32(SIGN_BIT), F32)
            nk = jnp.maximum(z, 0.0) + jnp.log(1.0 + jnp.exp2(minus_abs)) * LOG2E
            lb_ref[p, rows, :] = z - nk
            if masked:
                nk = jnp.where(strictly_earlier, nk, 0.0)
            nk_ref[p, rows, :] = nk.astype(BF16)
            carry = run_ref[rows, :]
            ca_ref[p, rows, :] = carry
            run_ref[rows, :] = carry - jnp.sum(nk, axis=-1, keepdims=True)

    def stage_b0(p):
        lt_ref[p] = _dot(nk_ref[p], tri_ref[...]) + lb_ref[p] + ca_ref[p]

    def stage_b1(n, p, masked):
        for rows in head_rows:
            w = jnp.exp2(lt_ref[p, rows, :])
            if masked:
                w = jnp.where(strictly_earlier, w, 0.0)
            w_ref[p, :, rows] = w.astype(BF16)
        v_blk = v_ref[0, pl.ds(pl.multiple_of((qi - n) * t, t), t), :]
        v_zero = jnp.zeros_like(v_blk)
        for h, rows in enumerate(head_rows):
            vs_ref[p, rows, :] = jnp.where(lane_head == h, v_blk, v_zero)

    def stage_c(p):
        acc_ref[...] += _dot(w_ref[p], vs_ref[p])

    def tick(n, p, first=0):
        stage_b1(n - 3, 1 - p, masked=False)
        stage_c(1 - p)
        if first <= 2:
            stage_b0(p)
        if first <= 1:
            stage_a1(1 - p, masked=False)
        if first <= 0:
            stage_a0(n, p)

    depth = 3

    @pl.when(n_blocks <= depth)
    def _():
        stage_a0(0, 0)
        stage_a1(0, masked=True)
        stage_b0(0)
        stage_b1(0, 0, masked=True)
        stage_c(0)

        def one_block(n, _):
            stage_a0(n, 0)
            stage_a1(0, masked=False)
            stage_b0(0)
            stage_b1(n, 0, masked=False)
            stage_c(0)
            return 0

        lax.fori_loop(1, n_blocks, one_block, 0)

    @pl.when(n_blocks > depth)
    def _():
        stage_a0(0, 0)
        stage_a1(0, masked=True)
        stage_a0(1, 1)
        stage_b0(0)
        stage_a1(1, masked=False)
        stage_a0(2, 0)
        stage_b1(0, 0, masked=True)
        stage_c(0)
        stage_b0(1)
        stage_a1(0, masked=False)
        stage_a0(3, 1)

        steady = n_blocks - depth - 1

        def tick_pair(i, _):
            n = depth + 1 + 2 * i
            tick(n, 0)
            tick(n + 1, 1)
            return 0

        lax.fori_loop(0, steady // 2, tick_pair, 0)

        @pl.when(steady % 2 == 1)
        def _():
            tick(n_blocks - 1, 0)

        for parity in (0, 1):
            @pl.when(n_blocks % 2 == parity)
            def _():
                for k in range(1, depth + 1):
                    tick(n_blocks - 1 + k, (parity + k - 1) % 2, first=k)

    o_ref[0] = acc_ref[...].astype(o_ref.dtype)


def _attn_prompt(q, ktb, v, bias):
    b, s, _ = q.shape
    t = ATTN_BLOCK
    idx = jnp.arange(t)
    tri = -(idx[:, None] > idx[None, :]).astype(BF16)
    stack = HEADS_PER_STEP * t
    return pl.pallas_call(
        _attn_prompt_kernel,
        grid=(b, SB_WIDTH // GROUP_LANES, s // t),
        in_specs=[
            pl.BlockSpec(memory_space=pltpu.SMEM),
            pl.BlockSpec((1, t, GROUP_LANES), lambda bi, g, qi: (bi, qi, g)),
            pl.BlockSpec((1, s // t, GROUP_LANES, t), lambda bi, g, qi: (bi, 0, g, 0)),
            pl.BlockSpec((1, s, GROUP_LANES), lambda bi, g, qi: (bi, 0, g)),
            _const_spec((t, t)),
        ],
        out_specs=pl.BlockSpec((1, t, GROUP_LANES), lambda bi, g, qi: (bi, qi, g)),
        out_shape=jax.ShapeDtypeStruct((b, s, SB_WIDTH), BF16),
        scratch_shapes=[
            pltpu.VMEM((stack, GROUP_LANES), BF16),
            pltpu.VMEM((2, stack, t), F32),
            pltpu.VMEM((2, stack, t), BF16),
            pltpu.VMEM((2, stack, t), F32),
            pltpu.VMEM((2, stack, 1), F32),
            pltpu.VMEM((2, stack, t), F32),
            pltpu.VMEM((2, t, stack), BF16),
            pltpu.VMEM((2, stack, GROUP_LANES), BF16),
            pltpu.VMEM((t, GROUP_LANES), F32),
            pltpu.VMEM((stack, 1), F32),
        ],
        compiler_params=pltpu.CompilerParams(
            dimension_semantics=("arbitrary", "arbitrary", "arbitrary"),
            vmem_limit_bytes=VMEM_LIMIT_BYTES),
        name="attn_prompt",
    )(bias, q, ktb, v, tri)


def _attn_sample_kernel(pt_ref, q_ref, kn_ref, vn_ref, bias_ref, tri_ref, *refs):
    del pt_ref
    g = PAGES_PER_STEP
    k_refs, v_refs = refs[:g], refs[g:2 * g]
    o_ref, qcol_ref, carry_ref, acc_ref = refs[2 * g:]
    step = pl.program_id(1)
    bias = bias_ref[...]

    def columns(row_ref):
        wide = jnp.broadcast_to(row_ref[0], (LANES, SB_WIDTH)).T
        return wide.reshape(SB_HEADS, SB_HEAD_DIM, LANES)

    head_row = lax.broadcasted_iota(jnp.int32, (SB_HEADS, LANES), 0)

    def head_rows(prod):
        out = jnp.zeros((SB_HEADS, LANES), F32)
        for h in range(SB_HEADS):
            out = jnp.where(head_row == h, jnp.sum(prod[h], axis=0, keepdims=True), out)
        return out

    @pl.when(step == 0)
    def _():
        qcol = columns(q_ref)
        qcol_ref[...] = qcol
        lane = lax.broadcasted_iota(jnp.int32, (SB_HEADS, LANES), 1)
        pos = jnp.zeros_like(lane)
        mask = (pos < pos) & (lane == 0)
        z = head_rows(qcol * columns(kn_ref)) + bias
        ls, log_keep = _log_sigmoid_pair(z)
        log_keep = jnp.where(mask, log_keep, 0.0)
        w = jnp.where(mask, jnp.exp(ls), 0.0)
        vcol = columns(vn_ref)
        for h in range(SB_HEADS):
            acc_ref[h] = w[h:h + 1, :] * vcol[h]
        carry_ref[...] = jnp.broadcast_to(jnp.sum(log_keep, axis=-1, keepdims=True), (SB_HEADS, LANES))

    qcol = qcol_ref[...]
    z = jnp.concatenate([head_rows(k_refs[i][0] * qcol) for i in range(g)], axis=0)
    z = z + jnp.concatenate([bias] * g, axis=0)
    ls, log_keep = _log_sigmoid_pair(z)
    later = _dot(log_keep.astype(BF16), tri_ref[...])
    page_sum = jnp.sum(log_keep, axis=-1, keepdims=True)
    carry = carry_ref[...]
    weights = []
    for i in range(g):
        rows = slice(i * SB_HEADS, (i + 1) * SB_HEADS)
        weights.append(jnp.exp(ls[rows] + later[rows] + carry))
        carry = carry + page_sum[rows]
    carry_ref[...] = carry
    for h in range(SB_HEADS):
        acc = acc_ref[h]
        for i in range(g):
            acc = acc + v_refs[i][0, h] * weights[i][h:h + 1, :]
        acc_ref[h] = acc

    @pl.when(step == pl.num_programs(1) - 1)
    def _():
        lane = lax.broadcasted_iota(jnp.int32, (SB_HEAD_DIM, LANES), 1)
        out = jnp.zeros((SB_HEAD_DIM, LANES), F32)
        for h in range(SB_HEADS):
            out = jnp.where(lane == h, jnp.sum(acc_ref[h], axis=-1, keepdims=True), out)
        o_ref[0] = out


def _attn_sample(q, k_new, v_new, cache_k, cache_v, page_table, bias):
    bd = q.shape[0]
    n_pages = page_table.shape[1]
    g = PAGES_PER_STEP
    ck = jnp.transpose(cache_k, (0, 2, 3, 1))
    cv = jnp.transpose(cache_v, (0, 2, 3, 1))
    idx = jnp.arange(PAGE_SIZE)
    tri = (idx[:, None] > idx[None, :]).astype(BF16)
    bias_rows = jnp.broadcast_to(bias.reshape(SB_HEADS, 1), (SB_HEADS, LANES))
    row3 = lambda a: a.reshape(bd, 1, SB_WIDTH)
    sample_spec = pl.BlockSpec((1, 1, SB_WIDTH), lambda b, s, pt: (b, 0, 0))
    const = lambda shape: pl.BlockSpec(shape, lambda b, s, pt: (0,) * len(shape))

    def page_spec(i):
        return pl.BlockSpec((1, SB_HEADS, SB_HEAD_DIM, PAGE_SIZE),
                            lambda b, s, pt: (pt[b, n_pages - 1 - (s * g + i)], 0, 0, 0))

    pages = [page_spec(i) for i in range(g)]
    out = pl.pallas_call(
        _attn_sample_kernel,
        grid_spec=pltpu.PrefetchScalarGridSpec(
            num_scalar_prefetch=1,
            grid=(bd, n_pages // g),
            in_specs=[sample_spec, sample_spec, sample_spec, const((SB_HEADS, LANES)),
                      const((PAGE_SIZE, PAGE_SIZE))] + pages + pages,
            out_specs=pl.BlockSpec((1, SB_HEAD_DIM, LANES), lambda b, s, pt: (b, 0, 0)),
            scratch_shapes=[pltpu.VMEM((SB_HEADS, SB_HEAD_DIM, LANES), F32),
                            pltpu.VMEM((SB_HEADS, LANES), F32),
                            pltpu.VMEM((SB_HEADS, SB_HEAD_DIM, PAGE_SIZE), F32)],
        ),
        out_shape=jax.ShapeDtypeStruct((bd, SB_HEAD_DIM, LANES), F32),
        compiler_params=pltpu.CompilerParams(
            dimension_semantics=("arbitrary", "arbitrary"), vmem_limit_bytes=VMEM_LIMIT_BYTES),
        name="attn_sample",
    )(page_table, row3(q), row3(k_new), row3(v_new), bias_rows, tri, *([ck] * g), *([cv] * g))
    return jnp.transpose(out[:, :, :SB_HEADS], (0, 2, 1)).reshape(bd, SB_WIDTH)


def kernel(x_prompt, x_sample, cache_k, cache_v, page_table, norm_ffn1, w_ffn1_in, w_ffn1_out, norm_mix, w_in, norm_gv, sb_bias, w_spatial, b_spatial, w_branch_a, w_branch_b, w_out, norm_ffn2, w_ffn2_in, w_ffn2_out, norm_final):
    b, s, _ = x_prompt.shape
    bd, t, _ = x_sample.shape
    depth = norm_ffn1.shape[0]
    xp = x_prompt.reshape(b * s, D_MODEL)
    xs = x_sample.reshape(bd * t, D_MODEL)
    tm_p, tm_s = 512, bd * t
    outs = [[] for _ in range(5)]
    for l in range(depth):
        w1i, w1o = w_ffn1_in[l].astype(BF16), w_ffn1_out[l].astype(BF16)
        w2i, w2o = w_ffn2_in[l].astype(BF16), w_ffn2_out[l].astype(BF16)
        wi = w_in[l].astype(BF16)
        wa, wb, wo = w_branch_a[l].astype(BF16), w_branch_b[l].astype(BF16), w_out[l].astype(BF16)
        last = l == depth - 1
        sg_w_p = w_spatial[l][:, :CHUNK, :CHUNK].reshape(SG_GROUPS // 2, 2 * CHUNK, CHUNK)
        sg_b_p = jnp.repeat(b_spatial[l][:, :CHUNK].T, SG_GROUP_DIM, axis=1)
        sg_w_s = jnp.repeat(w_spatial[l][:, 0, 0], SG_GROUP_DIM).reshape(1, SG_WIDTH)
        sg_b_s = jnp.repeat(b_spatial[l][:, 0], SG_GROUP_DIM).reshape(1, SG_WIDTH)

        xp = _ffn(xp, norm_ffn1[l], w1i, w1o, tm=tm_p)
        xs = _ffn(xs, norm_ffn1[l], w1i, w1o, tm=tm_s)

        qp, kpt, vpt, kpb, vpb, ya_p, ga_p, gb_p = _proj_prompt(
            xp, norm_mix[l], wi, norm_gv[l], sg_w_p, sg_b_p, batch=b, tm=tm_p)
        qs, ks, vs, ya_s, ga_s, gb_s, gvs = _proj_sample(
            xs, norm_mix[l], wi, norm_gv[l], sg_w_s, sg_b_s, tm=tm_s)

        yb_p = _attn_prompt(qp.reshape(b, s, SB_WIDTH), kpb, vpb.reshape(b, s, SB_WIDTH),
                            sb_bias[l]).reshape(b * s, SB_WIDTH)
        yb_s = _attn_sample(qs, ks, vs, cache_k[l], cache_v[l], page_table, sb_bias[l]).astype(BF16)

        xp = _merge(xp, ya_p, yb_p, ga_p, gb_p, wa, wb, wo, tm=tm_p)
        xs = _merge(xs, ya_s, yb_s, ga_s, gb_s, wa, wb, wo, tm=tm_s)

        xp = _ffn(xp, norm_ffn2[l], w2i, w2o, norm_final if last else None, tm=tm_p)
        xs = _ffn(xs, norm_ffn2[l], w2i, w2o, norm_final if last else None, tm=tm_s)

        to_rows = lambda a: jnp.transpose(a.reshape(b, SB_HEADS, SB_HEAD_DIM, s), (0, 3, 1, 2))
        outs[0].append(to_rows(kpt))
        outs[1].append(to_rows(vpt))
        outs[2].append(ks.reshape(bd, t, SB_HEADS, SB_HEAD_DIM))
        outs[3].append(vs.reshape(bd, t, SB_HEADS, SB_HEAD_DIM))
        outs[4].append(gvs.reshape(bd, t, SG_WIDTH))
    return (xp.reshape(b, s, D_MODEL), xs.reshape(bd, t, D_MODEL),
            jnp.stack(outs[0]), jnp.stack(outs[1]), jnp.stack(outs[2]), jnp.stack(outs[3]),
            jnp.stack(outs[4]))
```

```python
import functools

import jax
import jax.numpy as jnp
from jax import lax
from jax.experimental import pallas as pl
from jax.experimental.pallas import tpu as pltpu

D_MODEL = 1024
SB_HEADS = 8
SB_HEAD_DIM = 64
SB_WIDTH = SB_HEADS * SB_HEAD_DIM
SB_SCALE = SB_HEAD_DIM ** -0.5
SG_GROUPS = 8
SG_GROUP_DIM = 64
SG_WIDTH = SG_GROUPS * SG_GROUP_DIM
CHUNK = 128
PAGE_SIZE = 128
D_FF = 2816
EPS = 1e-6
LOG2E = 1.4426950408889634

ATTN_BLOCK = 256
HEADS_PER_STEP = 4
GROUP_LANES = HEADS_PER_STEP * SB_HEAD_DIM
SIGN_BIT = 0x80000000
PAGES_PER_STEP = 16

LANES = 128
VMEM_LIMIT_BYTES = 56 * 1024 * 1024

F32 = jnp.float32
BF16 = jnp.bfloat16


def _rms(x, g):
    return (x * lax.rsqrt(jnp.mean(x * x, axis=-1, keepdims=True) + EPS)) * g


def _dot(a, b):
    return jnp.dot(a, b, preferred_element_type=F32)


def _gelu(x):
    return 0.5 * x * (1.0 + lax.erf(x * (2.0 ** -0.5)))


def _log_sigmoid_pair(z):
    ls = jnp.minimum(z, 0.0) - jnp.log1p(jnp.exp(-jnp.abs(z)))
    return ls, ls - z


def _const_spec(shape):
    zeros = (0,) * len(shape)
    return pl.BlockSpec(shape, lambda *_: zeros)


def _ffn_kernel(x_ref, g_ref, win_ref, wout_ref, *rest, chunks, final_norm):
    if final_norm:
        gf_ref, o_ref = rest
    else:
        (o_ref,) = rest
    x = x_ref[...]
    xn = _rms(x, g_ref[...]).astype(BF16)
    acc = jnp.zeros(x.shape, F32)
    for lo, hi in chunks:
        gate = _dot(xn, win_ref[:, lo:hi])
        up = _dot(xn, win_ref[:, D_FF + lo:D_FF + hi])
        act = (gate * jax.nn.sigmoid(gate) * up).astype(BF16)
        acc = acc + _dot(act, wout_ref[lo:hi, :])
    y = x + 0.5 * acc
    if final_norm:
        y = _rms(y, gf_ref[...])
    o_ref[...] = y


def _ffn(x, g, w_in, w_out, g_final=None, *, tm):
    n = x.shape[0]
    chunks = ((0, 1536), (1536, D_FF))
    final_norm = g_final is not None
    in_specs = [
        pl.BlockSpec((tm, D_MODEL), lambda i: (i, 0)),
        _const_spec((1, D_MODEL)),
        _const_spec((D_MODEL, 2 * D_FF)),
        _const_spec((D_FF, D_MODEL)),
    ]
    args = [x, g.reshape(1, D_MODEL), w_in, w_out]
    if final_norm:
        in_specs.append(_const_spec((1, D_MODEL)))
        args.append(g_final.reshape(1, D_MODEL))
    return pl.pallas_call(
        functools.partial(_ffn_kernel, chunks=chunks, final_norm=final_norm),
        grid=(n // tm,),
        in_specs=in_specs,
        out_specs=pl.BlockSpec((tm, D_MODEL), lambda i: (i, 0)),
        out_shape=jax.ShapeDtypeStruct((n, D_MODEL), F32),
        compiler_params=pltpu.CompilerParams(
            dimension_semantics=("arbitrary",), vmem_limit_bytes=VMEM_LIMIT_BYTES),
        name="ffn_final" if final_norm else "ffn",
    )(*args)


def _mixer_inputs(x_ref, g_ref, w_ref, ggv_ref, ga_ref, gb_ref):
    h = _rms(x_ref[...], g_ref[...]).astype(BF16)

    def seg(lo, width):
        return _dot(h, w_ref[:, lo:lo + width])

    q = seg(0, SB_WIDTH)
    k = seg(SB_WIDTH, SB_WIDTH)
    v = seg(2 * SB_WIDTH, SB_WIDTH)
    u = _gelu(seg(3 * SB_WIDTH, SG_WIDTH))
    gv = _rms(_gelu(seg(3 * SB_WIDTH + SG_WIDTH, SG_WIDTH)), ggv_ref[...])
    base = 3 * SB_WIDTH + 2 * SG_WIDTH
    ga_ref[...] = jax.nn.sigmoid(seg(base, D_MODEL)).astype(BF16)
    gb_ref[...] = jax.nn.sigmoid(seg(base + D_MODEL, D_MODEL)).astype(BF16)
    return q, k, v, u, gv


def _proj_prompt_kernel(x_ref, g_ref, w_ref, ggv_ref, sg_w_ref, sg_b_ref,
                        q_ref, kt_ref, vt_ref, ktb_ref, vb_ref, ya_ref, ga_ref, gb_ref):
    tm = x_ref.shape[0]
    q, k, v, u, gv = _mixer_inputs(x_ref, g_ref, w_ref, ggv_ref, ga_ref, gb_ref)
    q_ref[...] = (q * (SB_SCALE * LOG2E)).astype(BF16)
    kt = k.T
    kt_ref[0] = kt
    for j in range(tm // ATTN_BLOCK):
        ktb_ref[0, j] = kt[:, j * ATTN_BLOCK:(j + 1) * ATTN_BLOCK].astype(BF16)
    vt_ref[0] = v.T
    vb_ref[...] = v.astype(BF16)

    gvb = gv.astype(BF16)
    row = lax.broadcasted_iota(jnp.int32, (2 * CHUNK, CHUNK), 0)
    col = lax.broadcasted_iota(jnp.int32, (2 * CHUNK, CHUNK), 1)
    causal = col <= jnp.where(row >= CHUNK, row - CHUNK, row)
    lane = lax.broadcasted_iota(jnp.int32, (CHUNK, LANES), 1)
    first_group = lane < SG_GROUP_DIM
    for p in range(SG_WIDTH // LANES):
        w_pair = jnp.where(causal, sg_w_ref[p], 0.0).astype(BF16)
        cols = slice(p * LANES, (p + 1) * LANES)
        for c in range(tm // CHUNK):
            rows = slice(c * CHUNK, (c + 1) * CHUNK)
            r = _dot(w_pair, gvb[rows, cols])
            sg = jnp.where(first_group, r[:CHUNK], r[CHUNK:]) + sg_b_ref[:, cols]
            ya_ref[rows, cols] = (u[rows, cols] * sg).astype(BF16)


def _proj_sample_kernel(x_ref, g_ref, w_ref, ggv_ref, sg_w_ref, sg_b_ref,
                        q_ref, k_ref, v_ref, ya_ref, ga_ref, gb_ref, gv_ref):
    q, k, v, u, gv = _mixer_inputs(x_ref, g_ref, w_ref, ggv_ref, ga_ref, gb_ref)
    q_ref[...] = q * SB_SCALE
    k_ref[...] = k
    v_ref[...] = v
    gv_ref[...] = gv
    ya_ref[...] = (u * (gv * sg_w_ref[...] + sg_b_ref[...])).astype(BF16)


def _proj_in_specs(tm, w, sg_w, sg_b):
    return [
        pl.BlockSpec((tm, D_MODEL), lambda i: (i, 0)),
        _const_spec((1, D_MODEL)),
        _const_spec(w.shape),
        _const_spec((1, SG_WIDTH)),
        _const_spec(sg_w.shape),
        _const_spec(sg_b.shape),
    ]


def _proj_prompt(x, g, w, g_gv, sg_w, sg_b, *, batch, tm):
    n = x.shape[0]
    s = n // batch
    tiles = s // tm
    t = ATTN_BLOCK
    row_spec = lambda width: pl.BlockSpec((tm, width), lambda i: (i, 0))
    transposed_spec = pl.BlockSpec((1, SB_WIDTH, tm), lambda i: (i // tiles, 0, i % tiles))
    out_shape = [
        jax.ShapeDtypeStruct((n, SB_WIDTH), BF16),
        jax.ShapeDtypeStruct((batch, SB_WIDTH, s), F32),
        jax.ShapeDtypeStruct((batch, SB_WIDTH, s), F32),
        jax.ShapeDtypeStruct((batch, s // t, SB_WIDTH, t), BF16),
        jax.ShapeDtypeStruct((n, SB_WIDTH), BF16),
        jax.ShapeDtypeStruct((n, SG_WIDTH), BF16),
        jax.ShapeDtypeStruct((n, D_MODEL), BF16),
        jax.ShapeDtypeStruct((n, D_MODEL), BF16),
    ]
    out_specs = [
        row_spec(SB_WIDTH), transposed_spec, transposed_spec,
        pl.BlockSpec((1, tm // t, SB_WIDTH, t), lambda i: (i // tiles, i % tiles, 0, 0)),
        row_spec(SB_WIDTH), row_spec(SG_WIDTH), row_spec(D_MODEL), row_spec(D_MODEL),
    ]
    return pl.pallas_call(
        _proj_prompt_kernel,
        grid=(n // tm,),
        in_specs=_proj_in_specs(tm, w, sg_w, sg_b),
        out_specs=out_specs,
        out_shape=out_shape,
        compiler_params=pltpu.CompilerParams(
            dimension_semantics=("arbitrary",), vmem_limit_bytes=VMEM_LIMIT_BYTES),
        name="proj_prompt",
    )(x, g.reshape(1, D_MODEL), w, g_gv.reshape(1, SG_WIDTH), sg_w, sg_b)


def _proj_sample(x, g, w, g_gv, sg_w, sg_b, *, tm):
    n = x.shape[0]
    row_spec = lambda width: pl.BlockSpec((tm, width), lambda i: (i, 0))
    widths = [SB_WIDTH, SB_WIDTH, SB_WIDTH, SG_WIDTH, D_MODEL, D_MODEL, SG_WIDTH]
    dtypes = [F32, F32, F32, BF16, BF16, BF16, F32]
    return pl.pallas_call(
        _proj_sample_kernel,
        grid=(n // tm,),
        in_specs=_proj_in_specs(tm, w, sg_w, sg_b),
        out_specs=[row_spec(wd) for wd in widths],
        out_shape=[jax.ShapeDtypeStruct((n, wd), dt) for wd, dt in zip(widths, dtypes)],
        compiler_params=pltpu.CompilerParams(
            dimension_semantics=("arbitrary",), vmem_limit_bytes=VMEM_LIMIT_BYTES),
        name="proj_sample",
    )(x, g.reshape(1, D_MODEL), w, g_gv.reshape(1, SG_WIDTH), sg_w, sg_b)


def _merge_kernel(x_ref, ya_ref, yb_ref, ga_ref, gb_ref, wa_ref, wb_ref, wo_ref, o_ref):
    m = (ga_ref[...].astype(F32) * _dot(ya_ref[...], wa_ref[...])
         + gb_ref[...].astype(F32) * _dot(yb_ref[...], wb_ref[...]))
    o_ref[...] = x_ref[...] + _dot(m.astype(BF16), wo_ref[...])


def _merge(x, ya, yb, ga, gb, wa, wb, wo, *, tm):
    n = x.shape[0]
    row_spec = lambda width: pl.BlockSpec((tm, width), lambda i: (i, 0))
    return pl.pallas_call(
        _merge_kernel,
        grid=(n // tm,),
        in_specs=[row_spec(D_MODEL), row_spec(SG_WIDTH), row_spec(SB_WIDTH), row_spec(D_MODEL),
                  row_spec(D_MODEL), _const_spec(wa.shape), _const_spec(wb.shape), _const_spec(wo.shape)],
        out_specs=row_spec(D_MODEL),
        out_shape=jax.ShapeDtypeStruct((n, D_MODEL), F32),
        compiler_params=pltpu.CompilerParams(
            dimension_semantics=("arbitrary",), vmem_limit_bytes=VMEM_LIMIT_BYTES),
        name="merge",
    )(x, ya, yb, ga, gb, wa, wb, wo)


def _attn_prompt_kernel(bias_ref, q_ref, kt_ref, v_ref, tri_ref, o_ref,
                        qs_ref, s_ref, nk_ref, lb_ref, ca_ref, lt_ref, w_ref, vs_ref,
                        acc_ref, run_ref):
    t = ATTN_BLOCK
    nh = HEADS_PER_STEP
    grp = pl.program_id(1)
    qi = pl.program_id(2)
    n_blocks = qi + 1
    lane_head = lax.broadcasted_iota(jnp.int32, (t, GROUP_LANES), 1) // SB_HEAD_DIM
    biases = [bias_ref[nh * grp + h] * LOG2E for h in range(nh)]
    row = lax.broadcasted_iota(jnp.int32, (t, t), 0)
    col = lax.broadcasted_iota(jnp.int32, (t, t), 1)
    strictly_earlier = col < row
    head_rows = [slice(h * t, (h + 1) * t) for h in range(nh)]

    q = q_ref[0]
    zero = jnp.zeros_like(q)
    for h, rows in enumerate(head_rows):
        qs_ref[rows, :] = jnp.where(lane_head == h, q, zero)
    acc_ref[...] = jnp.zeros_like(acc_ref)
    run_ref[...] = jnp.zeros_like(run_ref)

    def stage_a0(n, p):
        s_ref[p] = _dot(qs_ref[...], kt_ref[0, qi - n])

    def stage_a1(p, masked):
        for h, rows in enumerate(head_rows):
            z = s_ref[p, rows, :] + biases[h]
            minus_abs = lax.bitcast_convert_type(
                lax.bitcast_convert_type(z, jnp.uint32) | jnp.uint32(SIGN_BIT), F32)
            nk = jnp.maximum(z, 0.0) + jnp.log(1.0 + jnp.exp2(minus_abs)) * LOG2E
            lb_ref[p, rows, :] = z - nk
            if masked:
                nk = jnp.where(strictly_earlier, nk, 0.0)
            nk_ref[p, rows, :] = nk.astype(BF16)
            carry = run_ref[rows, :]
            ca_ref[p, rows, :] = carry
            run_ref[rows, :] = carry - jnp.sum(nk, axis=-1, keepdims=True)

    def stage_b0(p):
        lt_ref[p] = _dot(nk_ref[p], tri_ref[...]) + lb_ref[p] + ca_ref[p]

    def stage_b1(n, p, masked):
        for rows in head_rows:
            w = jnp.exp2(lt_ref[p, rows, :])
            if masked:
                w = jnp.where(strictly_earlier, w, 0.0)
            w_ref[p, :, rows] = w.astype(BF16)
        v_blk = v_ref[0, pl.ds(pl.multiple_of((qi - n) * t, t), t), :]
        v_zero = jnp.zeros_like(v_blk)
        for h, rows in enumerate(head_rows):
            vs_ref[p, rows, :] = jnp.where(lane_head == h, v_blk, v_zero)

    def stage_c(p):
        acc_ref[...] += _dot(w_ref[p], vs_ref[p])

    def tick(n, p, first=0):
        if first <= 2:
            stage_b0(p)
        if first <= 0:
            stage_a0(n, p)
        stage_b1(n - 3, 1 - p, masked=False)
        stage_c(1 - p)
        if first <= 1:
            stage_a1(1 - p, masked=False)

    depth = 3

    @pl.when(n_blocks <= depth)
    def _():
        stage_a0(0, 0)
        stage_a1(0, masked=True)
        stage_b0(0)
        stage_b1(0, 0, masked=True)
        stage_c(0)

        def one_block(n, _):
            stage_a0(n, 0)
            stage_a1(0, masked=False)
            stage_b0(0)
            stage_b1(n, 0, masked=False)
            stage_c(0)
            return 0

        lax.fori_loop(1, n_blocks, one_block, 0)

    @pl.when(n_blocks > depth)
    def _():
        stage_a0(0, 0)
        stage_a1(0, masked=True)
        stage_b0(0)
        stage_b1(0, 0, masked=True)
        stage_c(0)
        stage_a0(1, 1)
        stage_a1(1, masked=False)
        stage_b0(1)
        stage_a0(2, 0)
        stage_a1(0, masked=False)
        stage_a0(3, 1)

        steady = n_blocks - depth - 1

        def tick_pair(i, _):
            n = depth + 1 + 2 * i
            tick(n, 0)
            tick(n + 1, 1)
            return 0

        lax.fori_loop(0, steady // 2, tick_pair, 0)

        @pl.when(steady % 2 == 1)
        def _():
            tick(n_blocks - 1, 0)

        for parity in (0, 1):
            @pl.when(n_blocks % 2 == parity)
            def _():
                for k in range(1, depth + 1):
                    tick(n_blocks - 1 + k, (parity + k - 1) % 2, first=k)

    o_ref[0] = acc_ref[...].astype(o_ref.dtype)


def _attn_prompt(q, ktb, v, bias):
    b, s, _ = q.shape
    t = ATTN_BLOCK
    idx = jnp.arange(t)
    tri = -(idx[:, None] > idx[None, :]).astype(BF16)
    stack = HEADS_PER_STEP * t
    return pl.pallas_call(
        _attn_prompt_kernel,
        grid=(b, SB_WIDTH // GROUP_LANES, s // t),
        in_specs=[
            pl.BlockSpec(memory_space=pltpu.SMEM),
            pl.BlockSpec((1, t, GROUP_LANES), lambda bi, g, qi: (bi, qi, g)),
            pl.BlockSpec((1, s // t, GROUP_LANES, t), lambda bi, g, qi: (bi, 0, g, 0)),
            pl.BlockSpec((1, s, GROUP_LANES), lambda bi, g, qi: (bi, 0, g)),
            _const_spec((t, t)),
        ],
        out_specs=pl.BlockSpec((1, t, GROUP_LANES), lambda bi, g, qi: (bi, qi, g)),
        out_shape=jax.ShapeDtypeStruct((b, s, SB_WIDTH), BF16),
        scratch_shapes=[
            pltpu.VMEM((stack, GROUP_LANES), BF16),
            pltpu.VMEM((2, stack, t), F32),
            pltpu.VMEM((2, stack, t), BF16),
            pltpu.VMEM((2, stack, t), F32),
            pltpu.VMEM((2, stack, 1), F32),
            pltpu.VMEM((2, stack, t), F32),
            pltpu.VMEM((2, t, stack), BF16),
            pltpu.VMEM((2, stack, GROUP_LANES), BF16),
            pltpu.VMEM((t, GROUP_LANES), F32),
            pltpu.VMEM((stack, 1), F32),
        ],
        compiler_params=pltpu.CompilerParams(
            dimension_semantics=("arbitrary", "arbitrary", "arbitrary"),
            vmem_limit_bytes=VMEM_LIMIT_BYTES),
        name="attn_prompt",
    )(bias, q, ktb, v, tri)


def _attn_sample_kernel(pt_ref, q_ref, kn_ref, vn_ref, bias_ref, tri_ref, *refs):
    del pt_ref
    g = PAGES_PER_STEP
    k_refs, v_refs = refs[:g], refs[g:2 * g]
    o_ref, qcol_ref, carry_ref, acc_ref = refs[2 * g:]
    step = pl.program_id(1)
    bias = bias_ref[...]

    def columns(row_ref):
        wide = jnp.broadcast_to(row_ref[0], (LANES, SB_WIDTH)).T
        return wide.reshape(SB_HEADS, SB_HEAD_DIM, LANES)

    head_row = lax.broadcasted_iota(jnp.int32, (SB_HEADS, LANES), 0)

    def head_rows(prod):
        out = jnp.zeros((SB_HEADS, LANES), F32)
        for h in range(SB_HEADS):
            out = jnp.where(head_row == h, jnp.sum(prod[h], axis=0, keepdims=True), out)
        return out

    @pl.when(step == 0)
    def _():
        qcol = columns(q_ref)
        qcol_ref[...] = qcol
        lane = lax.broadcasted_iota(jnp.int32, (SB_HEADS, LANES), 1)
        pos = jnp.zeros_like(lane)
        mask = (pos < pos) & (lane == 0)
        z = head_rows(qcol * columns(kn_ref)) + bias
        ls, log_keep = _log_sigmoid_pair(z)
        log_keep = jnp.where(mask, log_keep, 0.0)
        w = jnp.where(mask, jnp.exp(ls), 0.0)
        vcol = columns(vn_ref)
        for h in range(SB_HEADS):
            acc_ref[h] = w[h:h + 1, :] * vcol[h]
        carry_ref[...] = jnp.broadcast_to(jnp.sum(log_keep, axis=-1, keepdims=True), (SB_HEADS, LANES))

    qcol = qcol_ref[...]
    z = jnp.concatenate([head_rows(k_refs[i][0] * qcol) for i in range(g)], axis=0)
    z = z + jnp.concatenate([bias] * g, axis=0)
    ls, log_keep = _log_sigmoid_pair(z)
    later = _dot(log_keep.astype(BF16), tri_ref[...])
    page_sum = jnp.sum(log_keep, axis=-1, keepdims=True)
    carry = carry_ref[...]
    weights = []
    for i in range(g):
        rows = slice(i * SB_HEADS, (i + 1) * SB_HEADS)
        weights.append(jnp.exp(ls[rows] + later[rows] + carry))
        carry = carry + page_sum[rows]
    carry_ref[...] = carry
    for h in range(SB_HEADS):
        acc = acc_ref[h]
        for i in range(g):
            acc = acc + v_refs[i][0, h] * weights[i][h:h + 1, :]
        acc_ref[h] = acc

    @pl.when(step == pl.num_programs(1) - 1)
    def _():
        lane = lax.broadcasted_iota(jnp.int32, (SB_HEAD_DIM, LANES), 1)
        out = jnp.zeros((SB_HEAD_DIM, LANES), F32)
        for h in range(SB_HEADS):
            out = jnp.where(lane == h, jnp.sum(acc_ref[h], axis=-1, keepdims=True), out)
        o_ref[0] = out


def _attn_sample(q, k_new, v_new, cache_k, cache_v, page_table, bias):
    bd = q.shape[0]
    n_pages = page_table.shape[1]
    g = PAGES_PER_STEP
    ck = jnp.transpose(cache_k, (0, 2, 3, 1))
    cv = jnp.transpose(cache_v, (0, 2, 3, 1))
    idx = jnp.arange(PAGE_SIZE)
    tri = (idx[:, None] > idx[None, :]).astype(BF16)
    bias_rows = jnp.broadcast_to(bias.reshape(SB_HEADS, 1), (SB_HEADS, LANES))
    row3 = lambda a: a.reshape(bd, 1, SB_WIDTH)
    sample_spec = pl.BlockSpec((1, 1, SB_WIDTH), lambda b, s, pt: (b, 0, 0))
    const = lambda shape: pl.BlockSpec(shape, lambda b, s, pt: (0,) * len(shape))

    def page_spec(i):
        return pl.BlockSpec((1, SB_HEADS, SB_HEAD_DIM, PAGE_SIZE),
                            lambda b, s, pt: (pt[b, n_pages - 1 - (s * g + i)], 0, 0, 0))

    pages = [page_spec(i) for i in range(g)]
    out = pl.pallas_call(
        _attn_sample_kernel,
        grid_spec=pltpu.PrefetchScalarGridSpec(
            num_scalar_prefetch=1,
            grid=(bd, n_pages // g),
            in_specs=[sample_spec, sample_spec, sample_spec, const((SB_HEADS, LANES)),
                      const((PAGE_SIZE, PAGE_SIZE))] + pages + pages,
            out_specs=pl.BlockSpec((1, SB_HEAD_DIM, LANES), lambda b, s, pt: (b, 0, 0)),
            scratch_shapes=[pltpu.VMEM((SB_HEADS, SB_HEAD_DIM, LANES), F32),
                            pltpu.VMEM((SB_HEADS, LANES), F32),
                            pltpu.VMEM((SB_HEADS, SB_HEAD_DIM, PAGE_SIZE), F32)],
        ),
        out_shape=jax.ShapeDtypeStruct((bd, SB_HEAD_DIM, LANES), F32),
        compiler_params=pltpu.CompilerParams(
            dimension_semantics=("arbitrary", "arbitrary"), vmem_limit_bytes=VMEM_LIMIT_BYTES),
        name="attn_sample",
    )(page_table, row3(q), row3(k_new), row3(v_new), bias_rows, tri, *([ck] * g), *([cv] * g))
    return jnp.transpose(out[:, :, :SB_HEADS], (0, 2, 1)).reshape(bd, SB_WIDTH)


def kernel(x_prompt, x_sample, cache_k, cache_v, page_table, norm_ffn1, w_ffn1_in, w_ffn1_out, norm_mix, w_in, norm_gv, sb_bias, w_spatial, b_spatial, w_branch_a, w_branch_b, w_out, norm_ffn2, w_ffn2_in, w_ffn2_out, norm_final):
    b, s, _ = x_prompt.shape
    bd, t, _ = x_sample.shape
    depth = norm_ffn1.shape[0]
    xp = x_prompt.reshape(b * s, D_MODEL)
    xs = x_sample.reshape(bd * t, D_MODEL)
    tm_p, tm_s = 512, bd * t
    outs = [[] for _ in range(5)]
    for l in range(depth):
        w1i, w1o = w_ffn1_in[l].astype(BF16), w_ffn1_out[l].astype(BF16)
        w2i, w2o = w_ffn2_in[l].astype(BF16), w_ffn2_out[l].astype(BF16)
        wi = w_in[l].astype(BF16)
        wa, wb, wo = w_branch_a[l].astype(BF16), w_branch_b[l].astype(BF16), w_out[l].astype(BF16)
        last = l == depth - 1
        sg_w_p = w_spatial[l][:, :CHUNK, :CHUNK].reshape(SG_GROUPS // 2, 2 * CHUNK, CHUNK)
        sg_b_p = jnp.repeat(b_spatial[l][:, :CHUNK].T, SG_GROUP_DIM, axis=1)
        sg_w_s = jnp.repeat(w_spatial[l][:, 0, 0], SG_GROUP_DIM).reshape(1, SG_WIDTH)
        sg_b_s = jnp.repeat(b_spatial[l][:, 0], SG_GROUP_DIM).reshape(1, SG_WIDTH)

        xp = _ffn(xp, norm_ffn1[l], w1i, w1o, tm=tm_p)
        xs = _ffn(xs, norm_ffn1[l], w1i, w1o, tm=tm_s)

        qp, kpt, vpt, kpb, vpb, ya_p, ga_p, gb_p = _proj_prompt(
            xp, norm_mix[l], wi, norm_gv[l], sg_w_p, sg_b_p, batch=b, tm=tm_p)
        qs, ks, vs, ya_s, ga_s, gb_s, gvs = _proj_sample(
            xs, norm_mix[l], wi, norm_gv[l], sg_w_s, sg_b_s, tm=tm_s)

        yb_p = _attn_prompt(qp.reshape(b, s, SB_WIDTH), kpb, vpb.reshape(b, s, SB_WIDTH),
                            sb_bias[l]).reshape(b * s, SB_WIDTH)
        yb_s = _attn_sample(qs, ks, vs, cache_k[l], cache_v[l], page_table, sb_bias[l]).astype(BF16)

        xp = _merge(xp, ya_p, yb_p, ga_p, gb_p, wa, wb, wo, tm=tm_p)
        xs = _merge(xs, ya_s, yb_s, ga_s, gb_s, wa, wb, wo, tm=tm_s)

        xp = _ffn(xp, norm_ffn2[l], w2i, w2o, norm_final if last else None, tm=tm_p)
        xs = _ffn(xs, norm_ffn2[l], w2i, w2o, norm_final if last else None, tm=tm_s)

        to_rows = lambda a: jnp.transpose(a.reshape(b, SB_HEADS, SB_HEAD_DIM, s), (0, 3, 1, 2))
        outs[0].append(to_rows(kpt))
        outs[1].append(to_rows(vpt))
        outs[2].append(ks.reshape(bd, t, SB_HEADS, SB_HEAD_DIM))
        outs[3].append(vs.reshape(bd, t, SB_HEADS, SB_HEAD_DIM))
        outs[4].append(gvs.reshape(bd, t, SG_WIDTH))
    return (xp.reshape(b, s, D_MODEL), xs.reshape(bd, t, D_MODEL),
            jnp.stack(outs[0]), jnp.stack(outs[1]), jnp.stack(outs[2]), jnp.stack(outs[3]),
            jnp.stack(outs[4]))
```

```python
import functools

import jax
import jax.numpy as jnp
from jax import lax
from jax.experimental import pallas as pl
from jax.experimental.pallas import tpu as pltpu

D_MODEL = 1024
SB_HEADS = 8
SB_HEAD_DIM = 64
SB_WIDTH = SB_HEADS * SB_HEAD_DIM
SB_SCALE = SB_HEAD_DIM ** -0.5
SG_GROUPS = 8
SG_GROUP_DIM = 64
SG_WIDTH = SG_GROUPS * SG_GROUP_DIM
CHUNK = 128
PAGE_SIZE = 128
D_FF = 2816
EPS = 1e-6
LOG2E = 1.4426950408889634

ATTN_BLOCK = 256
HEADS_PER_STEP = 4
GROUP_LANES = HEADS_PER_STEP * SB_HEAD_DIM
SIGN_BIT = 0x80000000
PAGES_PER_STEP = 16

LANES = 128
VMEM_LIMIT_BYTES = 56 * 1024 * 1024

F32 = jnp.float32
BF16 = jnp.bfloat16


def _rms(x, g):
    return (x * lax.rsqrt(jnp.mean(x * x, axis=-1, keepdims=True) + EPS)) * g


def _dot(a, b):
    return jnp.dot(a, b, preferred_element_type=F32)


def _gelu(x):
    return 0.5 * x * (1.0 + lax.erf(x * (2.0 ** -0.5)))


def _log_sigmoid_pair(z):
    ls = jnp.minimum(z, 0.0) - jnp.log1p(jnp.exp(-jnp.abs(z)))
    return ls, ls - z


def _const_spec(shape):
    zeros = (0,) * len(shape)
    return pl.BlockSpec(shape, lambda *_: zeros)


def _ffn_kernel(x_ref, g_ref, win_ref, wout_ref, *rest, chunks, final_norm):
    if final_norm:
        gf_ref, o_ref = rest
    else:
        (o_ref,) = rest
    x = x_ref[...]
    xn = _rms(x, g_ref[...]).astype(BF16)
    acc = jnp.zeros(x.shape, F32)
    for lo, hi in chunks:
        gate = _dot(xn, win_ref[:, lo:hi])
        up = _dot(xn, win_ref[:, D_FF + lo:D_FF + hi])
        act = (gate * jax.nn.sigmoid(gate) * up).astype(BF16)
        acc = acc + _dot(act, wout_ref[lo:hi, :])
    y = x + 0.5 * acc
    if final_norm:
        y = _rms(y, gf_ref[...])
    o_ref[...] = y


def _ffn(x, g, w_in, w_out, g_final=None, *, tm):
    n = x.shape[0]
    chunks = ((0, 1536), (1536, D_FF))
    final_norm = g_final is not None
    in_specs = [
        pl.BlockSpec((tm, D_MODEL), lambda i: (i, 0)),
        _const_spec((1, D_MODEL)),
        _const_spec((D_MODEL, 2 * D_FF)),
        _const_spec((D_FF, D_MODEL)),
    ]
    args = [x, g.reshape(1, D_MODEL), w_in, w_out]
    if final_norm:
        in_specs.append(_const_spec((1, D_MODEL)))
        args.append(g_final.reshape(1, D_MODEL))
    return pl.pallas_call(
        functools.partial(_ffn_kernel, chunks=chunks, final_norm=final_norm),
        grid=(n // tm,),
        in_specs=in_specs,
        out_specs=pl.BlockSpec((tm, D_MODEL), lambda i: (i, 0)),
        out_shape=jax.ShapeDtypeStruct((n, D_MODEL), F32),
        compiler_params=pltpu.CompilerParams(
            dimension_semantics=("arbitrary",), vmem_limit_bytes=VMEM_LIMIT_BYTES),
        name="ffn_final" if final_norm else "ffn",
    )(*args)


def _mixer_inputs(x_ref, g_ref, w_ref, ggv_ref, ga_ref, gb_ref):
    h = _rms(x_ref[...], g_ref[...]).astype(BF16)

    def seg(lo, width):
        return _dot(h, w_ref[:, lo:lo + width])

    q = seg(0, SB_WIDTH)
    k = seg(SB_WIDTH, SB_WIDTH)
    v = seg(2 * SB_WIDTH, SB_WIDTH)
    u = _gelu(seg(3 * SB_WIDTH, SG_WIDTH))
    gv = _rms(_gelu(seg(3 * SB_WIDTH + SG_WIDTH, SG_WIDTH)), ggv_ref[...])
    base = 3 * SB_WIDTH + 2 * SG_WIDTH
    ga_ref[...] = jax.nn.sigmoid(seg(base, D_MODEL)).astype(BF16)
    gb_ref[...] = jax.nn.sigmoid(seg(base + D_MODEL, D_MODEL)).astype(BF16)
    return q, k, v, u, gv


def _proj_prompt_kernel(x_ref, g_ref, w_ref, ggv_ref, sg_w_ref, sg_b_ref,
                        q_ref, kt_ref, vt_ref, ktb_ref, vb_ref, ya_ref, ga_ref, gb_ref):
    tm = x_ref.shape[0]
    q, k, v, u, gv = _mixer_inputs(x_ref, g_ref, w_ref, ggv_ref, ga_ref, gb_ref)
    q_ref[...] = (q * (SB_SCALE * LOG2E)).astype(BF16)
    kt = k.T
    kt_ref[0] = kt
    for j in range(tm // ATTN_BLOCK):
        ktb_ref[0, j] = kt[:, j * ATTN_BLOCK:(j + 1) * ATTN_BLOCK].astype(BF16)
    vt_ref[0] = v.T
    vb_ref[...] = v.astype(BF16)

    gvb = gv.astype(BF16)
    row = lax.broadcasted_iota(jnp.int32, (2 * CHUNK, CHUNK), 0)
    col = lax.broadcasted_iota(jnp.int32, (2 * CHUNK, CHUNK), 1)
    causal = col <= jnp.where(row >= CHUNK, row - CHUNK, row)
    lane = lax.broadcasted_iota(jnp.int32, (CHUNK, LANES), 1)
    first_group = lane < SG_GROUP_DIM
    for p in range(SG_WIDTH // LANES):
        w_pair = jnp.where(causal, sg_w_ref[p], 0.0).astype(BF16)
        cols = slice(p * LANES, (p + 1) * LANES)
        for c in range(tm // CHUNK):
            rows = slice(c * CHUNK, (c + 1) * CHUNK)
            r = _dot(w_pair, gvb[rows, cols])
            sg = jnp.where(first_group, r[:CHUNK], r[CHUNK:]) + sg_b_ref[:, cols]
            ya_ref[rows, cols] = (u[rows, cols] * sg).astype(BF16)


def _proj_sample_kernel(x_ref, g_ref, w_ref, ggv_ref, sg_w_ref, sg_b_ref,
                        q_ref, k_ref, v_ref, ya_ref, ga_ref, gb_ref, gv_ref):
    q, k, v, u, gv = _mixer_inputs(x_ref, g_ref, w_ref, ggv_ref, ga_ref, gb_ref)
    q_ref[...] = q * SB_SCALE
    k_ref[...] = k
    v_ref[...] = v
    gv_ref[...] = gv
    ya_ref[...] = (u * (gv * sg_w_ref[...] + sg_b_ref[...])).astype(BF16)


def _proj_in_specs(tm, w, sg_w, sg_b):
    return [
        pl.BlockSpec((tm, D_MODEL), lambda i: (i, 0)),
        _const_spec((1, D_MODEL)),
        _const_spec(w.shape),
        _const_spec((1, SG_WIDTH)),
        _const_spec(sg_w.shape),
        _const_spec(sg_b.shape),
    ]


def _proj_prompt(x, g, w, g_gv, sg_w, sg_b, *, batch, tm):
    n = x.shape[0]
    s = n // batch
    tiles = s // tm
    t = ATTN_BLOCK
    row_spec = lambda width: pl.BlockSpec((tm, width), lambda i: (i, 0))
    transposed_spec = pl.BlockSpec((1, SB_WIDTH, tm), lambda i: (i // tiles, 0, i % tiles))
    out_shape = [
        jax.ShapeDtypeStruct((n, SB_WIDTH), BF16),
        jax.ShapeDtypeStruct((batch, SB_WIDTH, s), F32),
        jax.ShapeDtypeStruct((batch, SB_WIDTH, s), F32),
        jax.ShapeDtypeStruct((batch, s // t, SB_WIDTH, t), BF16),
        jax.ShapeDtypeStruct((n, SB_WIDTH), BF16),
        jax.ShapeDtypeStruct((n, SG_WIDTH), BF16),
        jax.ShapeDtypeStruct((n, D_MODEL), BF16),
        jax.ShapeDtypeStruct((n, D_MODEL), BF16),
    ]
    out_specs = [
        row_spec(SB_WIDTH), transposed_spec, transposed_spec,
        pl.BlockSpec((1, tm // t, SB_WIDTH, t), lambda i: (i // tiles, i % tiles, 0, 0)),
        row_spec(SB_WIDTH), row_spec(SG_WIDTH), row_spec(D_MODEL), row_spec(D_MODEL),
    ]
    return pl.pallas_call(
        _proj_prompt_kernel,
        grid=(n // tm,),
        in_specs=_proj_in_specs(tm, w, sg_w, sg_b),
        out_specs=out_specs,
        out_shape=out_shape,
        compiler_params=pltpu.CompilerParams(
            dimension_semantics=("arbitrary",), vmem_limit_bytes=VMEM_LIMIT_BYTES),
        name="proj_prompt",
    )(x, g.reshape(1, D_MODEL), w, g_gv.reshape(1, SG_WIDTH), sg_w, sg_b)


def _proj_sample(x, g, w, g_gv, sg_w, sg_b, *, tm):
    n = x.shape[0]
    row_spec = lambda width: pl.BlockSpec((tm, width), lambda i: (i, 0))
    widths = [SB_WIDTH, SB_WIDTH, SB_WIDTH, SG_WIDTH, D_MODEL, D_MODEL, SG_WIDTH]
    dtypes = [F32, F32, F32, BF16, BF16, BF16, F32]
    return pl.pallas_call(
        _proj_sample_kernel,
        grid=(n // tm,),
        in_specs=_proj_in_specs(tm, w, sg_w, sg_b),
        out_specs=[row_spec(wd) for wd in widths],
        out_shape=[jax.ShapeDtypeStruct((n, wd), dt) for wd, dt in zip(widths, dtypes)],
        compiler_params=pltpu.CompilerParams(
            dimension_semantics=("arbitrary",), vmem_limit_bytes=VMEM_LIMIT_BYTES),
        name="proj_sample",
    )(x, g.reshape(1, D_MODEL), w, g_gv.reshape(1, SG_WIDTH), sg_w, sg_b)


def _merge_kernel(x_ref, ya_ref, yb_ref, ga_ref, gb_ref, wa_ref, wb_ref, wo_ref, o_ref):
    m = (ga_ref[...].astype(F32) * _dot(ya_ref[...], wa_ref[...])
         + gb_ref[...].astype(F32) * _dot(yb_ref[...], wb_ref[...]))
    o_ref[...] = x_ref[...] + _dot(m.astype(BF16), wo_ref[...])


def _merge(x, ya, yb, ga, gb, wa, wb, wo, *, tm):
    n = x.shape[0]
    row_spec = lambda width: pl.BlockSpec((tm, width), lambda i: (i, 0))
    return pl.pallas_call(
        _merge_kernel,
        grid=(n // tm,),
        in_specs=[row_spec(D_MODEL), row_spec(SG_WIDTH), row_spec(SB_WIDTH), row_spec(D_MODEL),
                  row_spec(D_MODEL), _const_spec(wa.shape), _const_spec(wb.shape), _const_spec(wo.shape)],
        out_specs=row_spec(D_MODEL),
        out_shape=jax.ShapeDtypeStruct((n, D_MODEL), F32),
        compiler_params=pltpu.CompilerParams(
            dimension_semantics=("arbitrary",), vmem_limit_bytes=VMEM_LIMIT_BYTES),
        name="merge",
    )(x, ya, yb, ga, gb, wa, wb, wo)


def _attn_prompt_kernel(bias_ref, q_ref, kt_ref, v_ref, tri_ref, o_ref,
                        qs_ref, s_ref, nk_ref, lb_ref, ca_ref, lt_ref, w_ref, vs_ref,
                        acc_ref, run_ref):
    t = ATTN_BLOCK
    nh = HEADS_PER_STEP
    grp = pl.program_id(1)
    qi = pl.program_id(2)
    n_blocks = qi + 1
    lane_head = lax.broadcasted_iota(jnp.int32, (t, GROUP_LANES), 1) // SB_HEAD_DIM
    biases = [bias_ref[nh * grp + h] * LOG2E for h in range(nh)]
    row = lax.broadcasted_iota(jnp.int32, (t, t), 0)
    col = lax.broadcasted_iota(jnp.int32, (t, t), 1)
    strictly_earlier = col < row
    head_rows = [slice(h * t, (h + 1) * t) for h in range(nh)]

    q = q_ref[0]
    zero = jnp.zeros_like(q)
    for h, rows in enumerate(head_rows):
        qs_ref[rows, :] = jnp.where(lane_head == h, q, zero)
    acc_ref[...] = jnp.zeros_like(acc_ref)
    run_ref[...] = jnp.zeros_like(run_ref)

    def stage_a0(n, p):
        s_ref[p] = _dot(qs_ref[...], kt_ref[0, qi - n])

    def stage_a1(p, masked):
        for h, rows in enumerate(head_rows):
            z = s_ref[p, rows, :] + biases[h]
            minus_abs = lax.bitcast_convert_type(
                lax.bitcast_convert_type(z, jnp.uint32) | jnp.uint32(SIGN_BIT), F32)
            nk = jnp.maximum(z, 0.0) + jnp.log(1.0 + jnp.exp2(minus_abs)) * LOG2E
            lb_ref[p, rows, :] = z - nk
            if masked:
                nk = jnp.where(strictly_earlier, nk, 0.0)
            nk_ref[p, rows, :] = nk.astype(BF16)
            carry = run_ref[rows, :]
            ca_ref[p, rows, :] = carry
            run_ref[rows, :] = carry - jnp.sum(nk, axis=-1, keepdims=True)

    def stage_b0(p):
        later = _dot(nk_ref[p], tri_ref[...])
        carry = ca_ref[p]
        for lo in range(0, t, LANES):
            cols = slice(lo, lo + LANES)
            lt_ref[p, :, cols] = later[:, cols] + lb_ref[p, :, cols] + carry

    def stage_b1(n, p, masked):
        for rows in head_rows:
            w = jnp.exp2(lt_ref[p, rows, :])
            if masked:
                w = jnp.where(strictly_earlier, w, 0.0)
            w_ref[p, :, rows] = w.astype(BF16)
        v_blk = v_ref[0, pl.ds(pl.multiple_of((qi - n) * t, t), t), :]
        v_zero = jnp.zeros_like(v_blk)
        for h, rows in enumerate(head_rows):
            vs_ref[p, rows, :] = jnp.where(lane_head == h, v_blk, v_zero)

    def stage_c(p):
        acc_ref[...] += _dot(w_ref[p], vs_ref[p])

    def tick(n, p, first=0):
        if first <= 2:
            stage_b0(p)
        if first <= 0:
            stage_a0(n, p)
        stage_b1(n - 3, 1 - p, masked=False)
        stage_c(1 - p)
        if first <= 1:
            stage_a1(1 - p, masked=False)

    depth = 3

    @pl.when(n_blocks <= depth)
    def _():
        stage_a0(0, 0)
        stage_a1(0, masked=True)
        stage_b0(0)
        stage_b1(0, 0, masked=True)
        stage_c(0)

        def one_block(n, _):
            stage_a0(n, 0)
            stage_a1(0, masked=False)
            stage_b0(0)
            stage_b1(n, 0, masked=False)
            stage_c(0)
            return 0

        lax.fori_loop(1, n_blocks, one_block, 0)

    @pl.when(n_blocks > depth)
    def _():
        stage_a0(0, 0)
        stage_a1(0, masked=True)
        stage_b0(0)
        stage_b1(0, 0, masked=True)
        stage_c(0)
        stage_a0(1, 1)
        stage_a1(1, masked=False)
        stage_b0(1)
        stage_a0(2, 0)
        stage_a1(0, masked=False)
        stage_a0(3, 1)

        steady = n_blocks - depth - 1

        def tick_pair(i, _):
            n = depth + 1 + 2 * i
            tick(n, 0)
            tick(n + 1, 1)
            return 0

        lax.fori_loop(0, steady // 2, tick_pair, 0)

        @pl.when(steady % 2 == 1)
        def _():
            tick(n_blocks - 1, 0)

        for parity in (0, 1):
            @pl.when(n_blocks % 2 == parity)
            def _():
                for k in range(1, depth + 1):
                    tick(n_blocks - 1 + k, (parity + k - 1) % 2, first=k)

    o_ref[0] = acc_ref[...].astype(o_ref.dtype)


def _attn_prompt(q, ktb, v, bias):
    b, s, _ = q.shape
    t = ATTN_BLOCK
    idx = jnp.arange(t)
    tri = -(idx[:, None] > idx[None, :]).astype(BF16)
    stack = HEADS_PER_STEP * t
    return pl.pallas_call(
        _attn_prompt_kernel,
        grid=(b, SB_WIDTH // GROUP_LANES, s // t),
        in_specs=[
            pl.BlockSpec(memory_space=pltpu.SMEM),
            pl.BlockSpec((1, t, GROUP_LANES), lambda bi, g, qi: (bi, qi, g)),
            pl.BlockSpec((1, s // t, GROUP_LANES, t), lambda bi, g, qi: (bi, 0, g, 0)),
            pl.BlockSpec((1, s, GROUP_LANES), lambda bi, g, qi: (bi, 0, g)),
            _const_spec((t, t)),
        ],
        out_specs=pl.BlockSpec((1, t, GROUP_LANES), lambda bi, g, qi: (bi, qi, g)),
        out_shape=jax.ShapeDtypeStruct((b, s, SB_WIDTH), BF16),
        scratch_shapes=[
            pltpu.VMEM((stack, GROUP_LANES), BF16),
            pltpu.VMEM((2, stack, t), F32),
            pltpu.VMEM((2, stack, t), BF16),
            pltpu.VMEM((2, stack, t), F32),
            pltpu.VMEM((2, stack, LANES), F32),
            pltpu.VMEM((2, stack, t), F32),
            pltpu.VMEM((2, t, stack), BF16),
            pltpu.VMEM((2, stack, GROUP_LANES), BF16),
            pltpu.VMEM((t, GROUP_LANES), F32),
            pltpu.VMEM((stack, LANES), F32),
        ],
        compiler_params=pltpu.CompilerParams(
            dimension_semantics=("arbitrary", "arbitrary", "arbitrary"),
            vmem_limit_bytes=VMEM_LIMIT_BYTES),
        name="attn_prompt",
    )(bias, q, ktb, v, tri)


def _attn_sample_kernel(pt_ref, q_ref, kn_ref, vn_ref, bias_ref, tri_ref, *refs):
    del pt_ref
    g = PAGES_PER_STEP
    k_refs, v_refs = refs[:g], refs[g:2 * g]
    o_ref, qcol_ref, carry_ref, acc_ref = refs[2 * g:]
    step = pl.program_id(1)
    bias = bias_ref[...]

    def columns(row_ref):
        wide = jnp.broadcast_to(row_ref[0], (LANES, SB_WIDTH)).T
        return wide.reshape(SB_HEADS, SB_HEAD_DIM, LANES)

    head_row = lax.broadcasted_iota(jnp.int32, (SB_HEADS, LANES), 0)

    def head_rows(prod):
        out = jnp.zeros((SB_HEADS, LANES), F32)
        for h in range(SB_HEADS):
            out = jnp.where(head_row == h, jnp.sum(prod[h], axis=0, keepdims=True), out)
        return out

    @pl.when(step == 0)
    def _():
        qcol = columns(q_ref)
        qcol_ref[...] = qcol
        lane = lax.broadcasted_iota(jnp.int32, (SB_HEADS, LANES), 1)
        pos = jnp.zeros_like(lane)
        mask = (pos < pos) & (lane == 0)
        z = head_rows(qcol * columns(kn_ref)) + bias
        ls, log_keep = _log_sigmoid_pair(z)
        log_keep = jnp.where(mask, log_keep, 0.0)
        w = jnp.where(mask, jnp.exp(ls), 0.0)
        vcol = columns(vn_ref)
        for h in range(SB_HEADS):
            acc_ref[h] = w[h:h + 1, :] * vcol[h]
        carry_ref[...] = jnp.broadcast_to(jnp.sum(log_keep, axis=-1, keepdims=True), (SB_HEADS, LANES))

    qcol = qcol_ref[...]
    z = jnp.concatenate([head_rows(k_refs[i][0] * qcol) for i in range(g)], axis=0)
    z = z + jnp.concatenate([bias] * g, axis=0)
    ls, log_keep = _log_sigmoid_pair(z)
    later = _dot(log_keep.astype(BF16), tri_ref[...])
    page_sum = jnp.sum(log_keep, axis=-1, keepdims=True)
    carry = carry_ref[...]
    weights = []
    for i in range(g):
        rows = slice(i * SB_HEADS, (i + 1) * SB_HEADS)
        weights.append(jnp.exp(ls[rows] + later[rows] + carry))
        carry = carry + page_sum[rows]
    carry_ref[...] = carry
    for h in range(SB_HEADS):
        acc = acc_ref[h]
        for i in range(g):
            acc = acc + v_refs[i][0, h] * weights[i][h:h + 1, :]
        acc_ref[h] = acc

    @pl.when(step == pl.num_programs(1) - 1)
    def _():
        lane = lax.broadcasted_iota(jnp.int32, (SB_HEAD_DIM, LANES), 1)
        out = jnp.zeros((SB_HEAD_DIM, LANES), F32)
        for h in range(SB_HEADS):
            out = jnp.where(lane == h, jnp.sum(acc_ref[h], axis=-1, keepdims=True), out)
        o_ref[0] = out


def _attn_sample(q, k_new, v_new, cache_k, cache_v, page_table, bias):
    bd = q.shape[0]
    n_pages = page_table.shape[1]
    g = PAGES_PER_STEP
    ck = jnp.transpose(cache_k, (0, 2, 3, 1))
    cv = jnp.transpose(cache_v, (0, 2, 3, 1))
    idx = jnp.arange(PAGE_SIZE)
    tri = (idx[:, None] > idx[None, :]).astype(BF16)
    bias_rows = jnp.broadcast_to(bias.reshape(SB_HEADS, 1), (SB_HEADS, LANES))
    row3 = lambda a: a.reshape(bd, 1, SB_WIDTH)
    sample_spec = pl.BlockSpec((1, 1, SB_WIDTH), lambda b, s, pt: (b, 0, 0))
    const = lambda shape: pl.BlockSpec(shape, lambda b, s, pt: (0,) * len(shape))

    def page_spec(i):
        return pl.BlockSpec((1, SB_HEADS, SB_HEAD_DIM, PAGE_SIZE),
                            lambda b, s, pt: (pt[b, n_pages - 1 - (s * g + i)], 0, 0, 0))

    pages = [page_spec(i) for i in range(g)]
    out = pl.pallas_call(
        _attn_sample_kernel,
        grid_spec=pltpu.PrefetchScalarGridSpec(
            num_scalar_prefetch=1,
            grid=(bd, n_pages // g),
            in_specs=[sample_spec, sample_spec, sample_spec, const((SB_HEADS, LANES)),
                      const((PAGE_SIZE, PAGE_SIZE))] + pages + pages,
            out_specs=pl.BlockSpec((1, SB_HEAD_DIM, LANES), lambda b, s, pt: (b, 0, 0)),
            scratch_shapes=[pltpu.VMEM((SB_HEADS, SB_HEAD_DIM, LANES), F32),
                            pltpu.VMEM((SB_HEADS, LANES), F32),
                            pltpu.VMEM((SB_HEADS, SB_HEAD_DIM, PAGE_SIZE), F32)],
        ),
        out_shape=jax.ShapeDtypeStruct((bd, SB_HEAD_DIM, LANES), F32),
        compiler_params=pltpu.CompilerParams(
            dimension_semantics=("arbitrary", "arbitrary"), vmem_limit_bytes=VMEM_LIMIT_BYTES),
        name="attn_sample",
    )(page_table, row3(q), row3(k_new), row3(v_new), bias_rows, tri, *([ck] * g), *([cv] * g))
    return jnp.transpose(out[:, :, :SB_HEADS], (0, 2, 1)).reshape(bd, SB_WIDTH)


def kernel(x_prompt, x_sample, cache_k, cache_v, page_table, norm_ffn1, w_ffn1_in, w_ffn1_out, norm_mix, w_in, norm_gv, sb_bias, w_spatial, b_spatial, w_branch_a, w_branch_b, w_out, norm_ffn2, w_ffn2_in, w_ffn2_out, norm_final):
    b, s, _ = x_prompt.shape
    bd, t, _ = x_sample.shape
    depth = norm_ffn1.shape[0]
    xp = x_prompt.reshape(b * s, D_MODEL)
    xs = x_sample.reshape(bd * t, D_MODEL)
    tm_p, tm_s = 512, bd * t
    outs = [[] for _ in range(5)]
    for l in range(depth):
        w1i, w1o = w_ffn1_in[l].astype(BF16), w_ffn1_out[l].astype(BF16)
        w2i, w2o = w_ffn2_in[l].astype(BF16), w_ffn2_out[l].astype(BF16)
        wi = w_in[l].astype(BF16)
        wa, wb, wo = w_branch_a[l].astype(BF16), w_branch_b[l].astype(BF16), w_out[l].astype(BF16)
        last = l == depth - 1
        sg_w_p = w_spatial[l][:, :CHUNK, :CHUNK].reshape(SG_GROUPS // 2, 2 * CHUNK, CHUNK)
        sg_b_p = jnp.repeat(b_spatial[l][:, :CHUNK].T, SG_GROUP_DIM, axis=1)
        sg_w_s = jnp.repeat(w_spatial[l][:, 0, 0], SG_GROUP_DIM).reshape(1, SG_WIDTH)
        sg_b_s = jnp.repeat(b_spatial[l][:, 0], SG_GROUP_DIM).reshape(1, SG_WIDTH)

        xp = _ffn(xp, norm_ffn1[l], w1i, w1o, tm=tm_p)
        xs = _ffn(xs, norm_ffn1[l], w1i, w1o, tm=tm_s)

        qp, kpt, vpt, kpb, vpb, ya_p, ga_p, gb_p = _proj_prompt(
            xp, norm_mix[l], wi, norm_gv[l], sg_w_p, sg_b_p, batch=b, tm=tm_p)
        qs, ks, vs, ya_s, ga_s, gb_s, gvs = _proj_sample(
            xs, norm_mix[l], wi, norm_gv[l], sg_w_s, sg_b_s, tm=tm_s)

        yb_p = _attn_prompt(qp.reshape(b, s, SB_WIDTH), kpb, vpb.reshape(b, s, SB_WIDTH),
                            sb_bias[l]).reshape(b * s, SB_WIDTH)
        yb_s = _attn_sample(qs, ks, vs, cache_k[l], cache_v[l], page_table, sb_bias[l]).astype(BF16)

        xp = _merge(xp, ya_p, yb_p, ga_p, gb_p, wa, wb, wo, tm=tm_p)
        xs = _merge(xs, ya_s, yb_s, ga_s, gb_s, wa, wb, wo, tm=tm_s)

        xp = _ffn(xp, norm_ffn2[l], w2i, w2o, norm_final if last else None, tm=tm_p)
        xs = _ffn(xs, norm_ffn2[l], w2i, w2o, norm_final if last else None, tm=tm_s)

        to_rows = lambda a: jnp.transpose(a.reshape(b, SB_HEADS, SB_HEAD_DIM, s), (0, 3, 1, 2))
        outs[0].append(to_rows(kpt))
        outs[1].append(to_rows(vpt))
        outs[2].append(ks.reshape(bd, t, SB_HEADS, SB_HEAD_DIM))
        outs[3].append(vs.reshape(bd, t, SB_HEADS, SB_HEAD_DIM))
        outs[4].append(gvs.reshape(bd, t, SG_WIDTH))
    return (xp.reshape(b, s, D_MODEL), xs.reshape(bd, t, D_MODEL),
            jnp.stack(outs[0]), jnp.stack(outs[1]), jnp.stack(outs[2]), jnp.stack(outs[3]),
            jnp.stack(outs[4]))
```
